```python
import jax, jax.numpy as jnp
from jax import lax
import numpy as np

D_MODEL = 4096
BATCH = 4
SEQ = 2048
DEPTH = 2

D_MIX = D_MODEL
W_A = D_MIX // 2
W_B = D_MIX - W_A
H_A = 8
H_B = 8
D_B = W_B // H_B
CONV_K = 31
CHUNK = 128
N_MEM = 256
XA_HEADS = 4
XA_DHEAD = D_MODEL // XA_HEADS
D_FF = 4 * D_MODEL
EPS = 1e-6

kernel_name = "hybrid_conv_gmlp_memxattn_block"


def _rmsnorm(x, g):
    xf = x.astype(jnp.float32)
    y = xf * lax.rsqrt(jnp.mean(xf * xf, axis=-1, keepdims=True) + EPS)
    return (y * g.astype(jnp.float32)).astype(x.dtype)


def _group_layernorm(x, g, b, n_groups):
    shp = x.shape
    xf = x.astype(jnp.float32).reshape(*shp[:-1], n_groups, shp[-1] // n_groups)
    mu = jnp.mean(xf, axis=-1, keepdims=True)
    xc = xf - mu
    var = jnp.mean(xc * xc, axis=-1, keepdims=True)
    y = (xc * lax.rsqrt(var + EPS)).reshape(shp)
    return (y * g.astype(jnp.float32) + b.astype(jnp.float32)).astype(x.dtype)


def _causal_depthwise_conv(x, w, b):
    C = x.shape[-1]
    y = lax.conv_general_dilated(
        x, w[:, None, :].astype(x.dtype), window_strides=(1,),
        padding=[(CONV_K - 1, 0)], dimension_numbers=('NWC', 'WIO', 'NWC'),
        feature_group_count=C)
    return y + b.astype(x.dtype)


def _conformer_conv_group(z_a, conv_w, conv_b, ln_g, ln_b):
    a, gate = jnp.split(z_a, 2, axis=-1)
    h = a * jax.nn.sigmoid(gate)
    h = _causal_depthwise_conv(h, conv_w, conv_b)
    h = _group_layernorm(h, ln_g, ln_b, H_A)
    return jax.nn.silu(h)


def _gmlp_group(z_b, ln_g, ln_b, w_s, b_s):
    z_b = jax.nn.gelu(z_b)
    u, v = jnp.split(z_b, 2, axis=-1)
    v = _group_layernorm(v, ln_g, ln_b, H_B)
    B, S, C = v.shape
    vc = v.reshape(B, S // CHUNK, CHUNK, H_B, D_B)
    causal = jnp.tril(jnp.ones((CHUNK, CHUNK), dtype=bool))
    w = jnp.where(causal[None], w_s, 0.0).astype(v.dtype)
    s = jnp.einsum('hts,bcshd->bcthd', w, vc)
    s = s + b_s.T.astype(v.dtype)[None, None, :, :, None]
    return u * s.reshape(B, S, C)


def _memory_cross_attention(h, m, w_q, w_kv, w_o):
    B, S, _ = h.shape
    q = (h @ w_q).reshape(B, S, XA_HEADS, XA_DHEAD)
    k, v = jnp.split(m @ w_kv, 2, axis=-1)
    k = k.reshape(B, N_MEM, XA_HEADS, XA_DHEAD)
    v = v.reshape(B, N_MEM, XA_HEADS, XA_DHEAD)
    scores = jnp.einsum('bshd,bmhd->bhsm', q, k).astype(jnp.float32) * (XA_DHEAD ** -0.5)
    p = jax.nn.softmax(scores, axis=-1).astype(v.dtype)
    o = jnp.einsum('bhsm,bmhd->bshd', p, v).reshape(B, S, D_MODEL)
    return o @ w_o


def setup_inputs(seed: int = 0) -> dict:
    key = jax.random.key(seed)
    ks = jax.random.split(key, 24)
    f32 = jnp.float32

    def nrm(k, shape, scale):
        return jax.random.normal(k, shape, f32) * scale

    def gain(k, n):
        return 1.0 + nrm(k, (DEPTH, n), 0.02)

    L = DEPTH
    return {
        "x": nrm(ks[0], (BATCH, SEQ, D_MODEL), 1.0),
        "mem": nrm(ks[1], (BATCH, N_MEM, D_MODEL), 1.0),
        "g_pre_mix": gain(ks[2], D_MODEL),
        "w_in": nrm(ks[3], (L, D_MODEL, 2 * W_A + 2 * W_B), D_MODEL ** -0.5),
        "conv_w": nrm(ks[4], (L, CONV_K, W_A), CONV_K ** -0.5),
        "conv_b": nrm(ks[5], (L, W_A), 0.02),
        "ln_a_g": gain(ks[6], W_A),
        "ln_a_b": nrm(ks[7], (L, W_A), 0.02),
        "ln_v_g": gain(ks[8], W_B),
        "ln_v_b": nrm(ks[9], (L, W_B), 0.02),
        "w_spatial": nrm(ks[10], (L, H_B, CHUNK, CHUNK), 0.5 * CHUNK ** -0.5),
        "b_spatial": 1.0 + nrm(ks[11], (L, H_B, CHUNK), 0.02),
        "w_out": nrm(ks[12], (L, D_MIX, D_MODEL), D_MIX ** -0.5),
        "g_post_mix": gain(ks[13], D_MODEL),
        "g_pre_xa": gain(ks[14], D_MODEL),
        "g_mem": gain(ks[15], D_MODEL),
        "w_q": nrm(ks[16], (L, D_MODEL, D_MODEL), D_MODEL ** -0.5),
        "w_kv": nrm(ks[17], (L, D_MODEL, 2 * D_MODEL), D_MODEL ** -0.5),
        "w_o": nrm(ks[18], (L, D_MODEL, D_MODEL), D_MODEL ** -0.5),
        "g_post_xa": gain(ks[19], D_MODEL),
        "g_pre_mlp": gain(ks[20], D_MODEL),
        "w_up": nrm(ks[21], (L, D_MODEL, D_FF), D_MODEL ** -0.5),
        "w_down": nrm(ks[22], (L, D_FF, D_MODEL), D_FF ** -0.5),
        "g_post_mlp": gain(ks[23], D_MODEL),
    }


def reference(x, mem, g_pre_mix, w_in, conv_w, conv_b, ln_a_g, ln_a_b, ln_v_g, ln_v_b,
              w_spatial, b_spatial, w_out, g_post_mix, g_pre_xa, g_mem, w_q, w_kv, w_o,
              g_post_xa, g_pre_mlp, w_up, w_down, g_post_mlp):
    for l in range(DEPTH):
        h = _rmsnorm(x, g_pre_mix[l])
        z = h @ w_in[l]
        y_a = _conformer_conv_group(z[..., :2 * W_A], conv_w[l], conv_b[l],
                                    ln_a_g[l], ln_a_b[l])
        y_b = _gmlp_group(z[..., 2 * W_A:], ln_v_g[l], ln_v_b[l],
                          w_spatial[l], b_spatial[l])
        mix = jnp.concatenate([y_a, y_b], axis=-1) @ w_out[l]
        x = x + _rmsnorm(mix, g_post_mix[l])
        h = _rmsnorm(x, g_pre_xa[l])
        m = _rmsnorm(mem, g_mem[l])
        xa = _memory_cross_attention(h, m, w_q[l], w_kv[l], w_o[l])
        x = x + _rmsnorm(xa, g_post_xa[l])
        h = _rmsnorm(x, g_pre_mlp[l])
        ff = jnp.square(jax.nn.relu(h @ w_up[l])) @ w_down[l]
        x = x + _rmsnorm(ff, g_post_mlp[l])
    return x
```

```python
import functools

import jax
import jax.numpy as jnp
from jax import lax
from jax.experimental import pallas as pl
from jax.experimental.pallas import tpu as pltpu

F32 = jnp.float32
BF16 = jnp.bfloat16

EPS = 1e-6
H_A = 8
H_B = 8
CONV_K = 31
CHUNK = 128
N_MEM = 256
XA_HEADS = 4

V7X_VMEM_BYTES = 64 * 1024 * 1024
SUBLANES = 8
BF16_SUBLANES = 16
CONV_PAD = 32
CONV_ROWS = 64
CONV_TAIL = CONV_ROWS + CONV_PAD


def _vmem_limit(nbytes):
    return int(min(nbytes + 12 * 1024 * 1024, V7X_VMEM_BYTES - 4 * 1024 * 1024))


def _rmsnorm_rows_to(x_ref, g_ref, h_ref):
    rows = BF16_SUBLANES
    n = x_ref.shape[0] // rows

    def body(r, carry):
        sl = pl.ds(pl.multiple_of(r * rows, rows), rows)
        xf = x_ref[sl, :]
        ms = jnp.mean(xf * xf, axis=-1, keepdims=True)
        h_ref[sl, :] = (xf * lax.rsqrt(ms + EPS) * g_ref[...]).astype(h_ref.dtype)
        return carry

    lax.fori_loop(0, n, body, 0)


def _group_layernorm(v, g, b):
    mu = jnp.mean(v, axis=-1, keepdims=True)
    xc = v - mu
    var = jnp.mean(xc * xc, axis=-1, keepdims=True)
    return xc * lax.rsqrt(var + EPS) * g + b


def _norm_matmul_kernel(x_ref, g_ref, w_ref, o_ref, h_ref):
    @pl.when(pl.program_id(1) == 0)
    def _():
        _rmsnorm_rows_to(x_ref, g_ref, h_ref)

    o_ref[...] = jnp.dot(h_ref[...], w_ref[...],
                         preferred_element_type=F32).astype(o_ref.dtype)


def _norm_matmul(x, g, w, layer, *, tm, tn, out_dtype):
    m, d = x.shape
    n = w.shape[2]
    nbytes = 2 * tm * d * 4 + tm * d * 2 + 2 * d * tn * 2 + 2 * tm * tn * 4
    return pl.pallas_call(
        _norm_matmul_kernel,
        grid=(m // tm, n // tn),
        in_specs=[
            pl.BlockSpec((tm, d), lambda i, j: (i, 0)),
            pl.BlockSpec((1, d), lambda i, j: (0, 0)),
            pl.BlockSpec((None, d, tn), lambda i, j: (layer, 0, j)),
        ],
        out_specs=pl.BlockSpec((tm, tn), lambda i, j: (i, j)),
        out_shape=jax.ShapeDtypeStruct((m, n), out_dtype),
        scratch_shapes=[pltpu.VMEM((tm, d), BF16)],
        compiler_params=pltpu.CompilerParams(
            dimension_semantics=("arbitrary", "arbitrary"),
            vmem_limit_bytes=_vmem_limit(nbytes)),
        name="norm_matmul",
    )(x, g, w)


def _inproj_glu_kernel(x_ref, g_ref, wa_ref, wg_ref, o_ref, h_ref):
    @pl.when(pl.program_id(1) == 0)
    def _():
        _rmsnorm_rows_to(x_ref, g_ref, h_ref)

    h = h_ref[...]
    a = jnp.dot(h, wa_ref[...], preferred_element_type=F32)
    gate = jnp.dot(h, wg_ref[...], preferred_element_type=F32)
    o_ref[...] = a * jax.nn.sigmoid(gate)


def _inproj_glu(x, g, w_in, layer, *, w_a, tm, tn):
    m, d = x.shape
    gate_off = w_a // tn
    nbytes = 2 * tm * d * 4 + tm * d * 2 + 4 * d * tn * 2 + 2 * tm * tn * 4
    return pl.pallas_call(
        _inproj_glu_kernel,
        grid=(m // tm, w_a // tn),
        in_specs=[
            pl.BlockSpec((tm, d), lambda i, j: (i, 0)),
            pl.BlockSpec((1, d), lambda i, j: (0, 0)),
            pl.BlockSpec((None, d, tn), lambda i, j: (layer, 0, j)),
            pl.BlockSpec((None, d, tn), lambda i, j: (layer, 0, j + gate_off)),
        ],
        out_specs=pl.BlockSpec((tm, tn), lambda i, j: (i, j)),
        out_shape=jax.ShapeDtypeStruct((m, w_a), F32),
        scratch_shapes=[pltpu.VMEM((tm, d), BF16)],
        compiler_params=pltpu.CompilerParams(
            dimension_semantics=("arbitrary", "arbitrary"),
            vmem_limit_bytes=_vmem_limit(nbytes)),
        name="inproj_glu",
    )(x, g, w_in, w_in)


def _inproj_gmlp_kernel(x_ref, g_ref, wu_ref, wv_ref, lg_ref, lb_ref, ws_ref, bs_ref,
                        o_ref, h_ref):
    @pl.when(pl.program_id(1) == 0)
    def _():
        _rmsnorm_rows_to(x_ref, g_ref, h_ref)

    h = h_ref[...]
    u = jax.nn.gelu(jnp.dot(h, wu_ref[...], preferred_element_type=F32))
    v = jax.nn.gelu(jnp.dot(h, wv_ref[...], preferred_element_type=F32))
    vn = _group_layernorm(v, lg_ref[...], lb_ref[...]).astype(BF16)

    t_idx = lax.broadcasted_iota(jnp.int32, (CHUNK, CHUNK), 0)
    s_idx = lax.broadcasted_iota(jnp.int32, (CHUNK, CHUNK), 1)
    w_sp = jnp.where(t_idx >= s_idx, ws_ref[...], 0.0).astype(BF16)
    bias = bs_ref[...]
    for c in range(o_ref.shape[0] // CHUNK):
        rows = slice(c * CHUNK, (c + 1) * CHUNK)
        s = jnp.dot(w_sp, vn[rows, :], preferred_element_type=F32)
        o_ref[rows, :] = (u[rows, :] * (s + bias)).astype(o_ref.dtype)


def _inproj_gmlp(x, g, w_in, ln_g, ln_b, w_sp, b_sp, layer, *, w_a, w_b, tm):
    m, d = x.shape
    d_b = w_b // H_B
    u_off = 2 * w_a // d_b
    v_off = u_off + H_B
    nbytes = 2 * tm * d * 4 + tm * d * 2 + 4 * d * d_b * 2 + 2 * tm * d_b * 2
    return pl.pallas_call(
        _inproj_gmlp_kernel,
        grid=(m // tm, H_B),
        in_specs=[
            pl.BlockSpec((tm, d), lambda i, j: (i, 0)),
            pl.BlockSpec((1, d), lambda i, j: (0, 0)),
            pl.BlockSpec((None, d, d_b), lambda i, j: (layer, 0, j + u_off)),
            pl.BlockSpec((None, d, d_b), lambda i, j: (layer, 0, j + v_off)),
            pl.BlockSpec((1, d_b), lambda i, j: (0, j)),
            pl.BlockSpec((1, d_b), lambda i, j: (0, j)),
            pl.BlockSpec((None, None, CHUNK, CHUNK), lambda i, j: (layer, j, 0, 0)),
            pl.BlockSpec((None, CHUNK, 1), lambda i, j: (j, 0, 0)),
        ],
        out_specs=pl.BlockSpec((tm, d_b), lambda i, j: (i, j)),
        out_shape=jax.ShapeDtypeStruct((m, w_b), BF16),
        scratch_shapes=[pltpu.VMEM((tm, d), BF16)],
        compiler_params=pltpu.CompilerParams(
            dimension_semantics=("arbitrary", "arbitrary"),
            vmem_limit_bytes=_vmem_limit(nbytes)),
        name="inproj_gmlp",
    )(x, g, w_in, w_in, ln_g, ln_b, w_sp, b_sp)


def _conv_ln_silu_kernel(x_ref, w_ref, cb_ref, lg_ref, lb_ref, o_ref, p_ref):
    seq, c = x_ref.shape
    rows = CONV_ROWS
    p_ref[0, 0:CONV_PAD, :] = jnp.zeros((CONV_PAD, c), F32)
    p_ref[0, CONV_PAD + seq:, :] = jnp.zeros((CONV_TAIL, c), F32)
    p_ref[0, CONV_PAD:CONV_PAD + seq, :] = x_ref[...]

    def build(i, carry):
        t0 = pl.multiple_of(i * rows, rows)
        win = p_ref[0, pl.ds(t0, rows + SUBLANES), :]
        for r in range(1, SUBLANES):
            p_ref[r, pl.ds(t0, rows), :] = win[r:r + rows, :]
        return carry

    lax.fori_loop(0, (seq + rows) // rows, build, 0)
    shift = CONV_PAD - (CONV_K - 1)

    def body(i, carry):
        t0 = pl.multiple_of(i * rows, rows)
        acc = jnp.broadcast_to(cb_ref[...], (rows, c))
        for k in range(CONV_K):
            q, r = divmod(shift + k, SUBLANES)
            start = pl.multiple_of(t0 + SUBLANES * q, SUBLANES)
            acc = acc + w_ref[k:k + 1, :] * p_ref[r, pl.ds(start, rows), :]
        y = _group_layernorm(acc, lg_ref[...], lb_ref[...])
        o_ref[pl.ds(t0, rows), :] = (y * jax.nn.sigmoid(y)).astype(o_ref.dtype)
        return carry

    lax.fori_loop(0, seq // rows, body, 0)


def _conv_ln_silu(hglu, conv_w, conv_b, ln_g, ln_b, *, batch, seq):
    m, w_a = hglu.shape
    cg = w_a // H_A
    return pl.pallas_call(
        _conv_ln_silu_kernel,
        grid=(batch, H_A),
        in_specs=[
            pl.BlockSpec((seq, cg), lambda b, c: (b, c)),
            pl.BlockSpec((CONV_K, cg), lambda b, c: (0, c)),
            pl.BlockSpec((1, cg), lambda b, c: (0, c)),
            pl.BlockSpec((1, cg), lambda b, c: (0, c)),
            pl.BlockSpec((1, cg), lambda b, c: (0, c)),
        ],
        out_specs=pl.BlockSpec((seq, cg), lambda b, c: (b, c)),
        out_shape=jax.ShapeDtypeStruct((m, w_a), BF16),
        scratch_shapes=[pltpu.VMEM((SUBLANES, CONV_PAD + seq + CONV_TAIL, cg), F32)],
        compiler_params=pltpu.CompilerParams(
            dimension_semantics=("arbitrary", "arbitrary"),
            vmem_limit_bytes=_vmem_limit(
                SUBLANES * (CONV_PAD + seq + CONV_TAIL) * cg * 4 + 2 * seq * cg * 6)),
        name="conv_ln_silu",
    )(hglu, conv_w, conv_b, ln_g, ln_b)


def _norm_residual_epilogue(acc_ref, x_ref, g_ref, o_ref):
    nj, tm, tn = acc_ref.shape
    rows = 8
    inv_d = 1.0 / (nj * tn)

    def body(r, carry):
        sl = pl.ds(pl.multiple_of(r * rows, rows), rows)
        ss = jnp.zeros((rows, 1), F32)
        for jj in range(nj):
            a = acc_ref[jj, sl, :]
            ss = ss + jnp.sum(a * a, axis=-1, keepdims=True)
        rstd = lax.rsqrt(ss * inv_d + EPS)
        for jj in range(nj):
            cols = slice(jj * tn, (jj + 1) * tn)
            o_ref[sl, cols] = x_ref[sl, cols] + acc_ref[jj, sl, :] * rstd * g_ref[:, cols]
        return carry

    lax.fori_loop(0, tm // rows, body, 0)


def _proj_norm_res_kernel(*refs, n_a):
    a_refs = refs[:n_a]
    w_ref, x_ref, g_ref, o_ref, acc_ref = refs[n_a:]
    j = pl.program_id(1)
    acc = None
    off = 0
    for a_ref in a_refs:
        ka = a_ref.shape[1]
        part = jnp.dot(a_ref[...], w_ref[off:off + ka, :], preferred_element_type=F32)
        acc = part if acc is None else acc + part
        off += ka
    acc_ref[j] = acc

    @pl.when(j == pl.num_programs(1) - 1)
    def _():
        _norm_residual_epilogue(acc_ref, x_ref, g_ref, o_ref)


def _proj_norm_res(a_list, w, x, g, layer, *, tm, tn):
    m, d = x.shape
    k = w.shape[1]
    n_a = len(a_list)
    nbytes = (2 * tm * k * 2 + 2 * k * tn * 2 + tm * d * 4 + 2 * tm * d * 4 + tm * d * 4)
    a_specs = [pl.BlockSpec((tm, a.shape[1]), lambda i, j: (i, 0)) for a in a_list]
    return pl.pallas_call(
        functools.partial(_proj_norm_res_kernel, n_a=n_a),
        grid=(m // tm, d // tn),
        in_specs=a_specs + [
            pl.BlockSpec((None, k, tn), lambda i, j: (layer, 0, j)),
            pl.BlockSpec((tm, d), lambda i, j: (i, 0), pipeline_mode=pl.Buffered(1)),
            pl.BlockSpec((1, d), lambda i, j: (0, 0)),
        ],
        out_specs=pl.BlockSpec((tm, d), lambda i, j: (i, 0)),
        out_shape=jax.ShapeDtypeStruct((m, d), F32),
        scratch_shapes=[pltpu.VMEM((d // tn, tm, tn), F32)],
        compiler_params=pltpu.CompilerParams(
            dimension_semantics=("arbitrary", "arbitrary"),
            vmem_limit_bytes=_vmem_limit(nbytes)),
        name="proj_norm_res",
    )(*a_list, w, x, g)


def _xattn_kernel(x_ref, g_ref, wq_ref, k_ref, v_ref, o_ref, h_ref, *, scale):
    @pl.when(pl.program_id(1) == 0)
    def _():
        _rmsnorm_rows_to(x_ref, g_ref, h_ref)

    q = jnp.dot(h_ref[...], wq_ref[...], preferred_element_type=F32).astype(BF16)
    s = lax.dot_general(q, k_ref[...], (((1,), (1,)), ((), ())),
                        preferred_element_type=F32) * scale
    p = jnp.exp(s - jnp.max(s, axis=-1, keepdims=True))
    p = (p / jnp.sum(p, axis=-1, keepdims=True)).astype(BF16)
    o_ref[...] = jnp.dot(p, v_ref[...], preferred_element_type=F32).astype(o_ref.dtype)


def _xattn(x, g, w_q, kv, layer, *, seq, tm):
    m, d = x.shape
    dh = d // XA_HEADS
    nbytes = 2 * tm * d * 4 + tm * d * 2 + 2 * d * dh * 2 + 8 * N_MEM * dh * 2 + 2 * tm * dh * 2
    return pl.pallas_call(
        functools.partial(_xattn_kernel, scale=float(dh) ** -0.5),
        grid=(m // tm, XA_HEADS),
        in_specs=[
            pl.BlockSpec((tm, d), lambda i, j: (i, 0)),
            pl.BlockSpec((1, d), lambda i, j: (0, 0)),
            pl.BlockSpec((None, d, dh), lambda i, j: (layer, 0, j)),
            pl.BlockSpec((N_MEM, dh), lambda i, j: ((i * tm) // seq, j)),
            pl.BlockSpec((N_MEM, dh), lambda i, j: ((i * tm) // seq, j + XA_HEADS)),
        ],
        out_specs=pl.BlockSpec((tm, dh), lambda i, j: (i, j)),
        out_shape=jax.ShapeDtypeStruct((m, d), BF16),
        scratch_shapes=[pltpu.VMEM((tm, d), BF16)],
        compiler_params=pltpu.CompilerParams(
            dimension_semantics=("arbitrary", "arbitrary"),
            vmem_limit_bytes=_vmem_limit(nbytes)),
        name="xattn",
    )(x, g, w_q, kv, kv)


def _mlp_epilogue(x_ref, g_ref, o_ref):
    rows = 8

    def body(r, carry):
        sl = pl.ds(pl.multiple_of(r * rows, rows), rows)
        a = o_ref[sl, :]
        rstd = lax.rsqrt(jnp.mean(a * a, axis=-1, keepdims=True) + EPS)
        o_ref[sl, :] = x_ref[sl, :] + a * rstd * g_ref[...]
        return carry

    lax.fori_loop(0, o_ref.shape[0] // rows, body, 0)


def _mlp_kernel(x_ref, gpre_ref, wu_ref, wd_ref, gpost_ref, o_ref, h_ref):
    f = pl.program_id(1)

    @pl.when(f == 0)
    def _():
        _rmsnorm_rows_to(x_ref, gpre_ref, h_ref)

    u = jnp.dot(h_ref[...], wu_ref[...], preferred_element_type=F32)
    u = jnp.square(jnp.maximum(u, 0.0)).astype(BF16)
    part = jnp.dot(u, wd_ref[...], preferred_element_type=F32)

    @pl.when(f == 0)
    def _():
        o_ref[...] = part

    @pl.when(f > 0)
    def _():
        o_ref[...] += part

    @pl.when(f == pl.num_programs(1) - 1)
    def _():
        _mlp_epilogue(x_ref, gpost_ref, o_ref)


def _mlp(x, g_pre, w_up, w_down, g_post, layer, *, tm, tf):
    m, d = x.shape
    d_ff = w_up.shape[2]
    nbytes = tm * d * 4 + tm * d * 2 + 4 * d * tf * 2 + 2 * tm * d * 4 + tm * d * 4
    return pl.pallas_call(
        _mlp_kernel,
        grid=(m // tm, d_ff // tf),
        in_specs=[
            pl.BlockSpec((tm, d), lambda i, f: (i, 0), pipeline_mode=pl.Buffered(1)),
            pl.BlockSpec((1, d), lambda i, f: (0, 0)),
            pl.BlockSpec((None, d, tf), lambda i, f: (layer, 0, f)),
            pl.BlockSpec((None, tf, d), lambda i, f: (layer, f, 0)),
            pl.BlockSpec((1, d), lambda i, f: (0, 0)),
        ],
        out_specs=pl.BlockSpec((tm, d), lambda i, f: (i, 0)),
        out_shape=jax.ShapeDtypeStruct((m, d), F32),
        scratch_shapes=[pltpu.VMEM((tm, d), BF16)],
        compiler_params=pltpu.CompilerParams(
            dimension_semantics=("arbitrary", "arbitrary"),
            vmem_limit_bytes=_vmem_limit(nbytes)),
        name="mlp",
    )(x, g_pre, w_up, w_down, g_post)


def kernel(x, mem, g_pre_mix, w_in, conv_w, conv_b, ln_a_g, ln_a_b, ln_v_g, ln_v_b, w_spatial, b_spatial, w_out, g_post_mix, g_pre_xa, g_mem, w_q, w_kv, w_o, g_post_xa, g_pre_mlp, w_up, w_down, g_post_mlp):
    batch, seq, d = x.shape
    depth = w_in.shape[0]
    w_a = conv_w.shape[2]
    w_b = ln_v_g.shape[1]
    m = batch * seq

    w_in_b = w_in.astype(BF16)
    w_out_b = w_out.astype(BF16)
    w_q_b = w_q.astype(BF16)
    w_kv_b = w_kv.astype(BF16)
    w_o_b = w_o.astype(BF16)
    w_up_b = w_up.astype(BF16)
    w_down_b = w_down.astype(BF16)

    xf = x.reshape(m, d)
    memf = mem.reshape(batch * N_MEM, d)

    def row(p, l):
        return p[l].reshape(1, -1)

    for l in range(depth):
        g1 = row(g_pre_mix, l)
        hglu = _inproj_glu(xf, g1, w_in_b, l, w_a=w_a, tm=512, tn=512)
        y_a = _conv_ln_silu(hglu, conv_w[l], row(conv_b, l), row(ln_a_g, l), row(ln_a_b, l),
                            batch=batch, seq=seq)
        y_b = _inproj_gmlp(xf, g1, w_in_b, row(ln_v_g, l), row(ln_v_b, l), w_spatial,
                           b_spatial[l].reshape(H_B, CHUNK, 1), l, w_a=w_a, w_b=w_b, tm=512)
        xf = _proj_norm_res([y_a, y_b], w_out_b, xf, row(g_post_mix, l), l, tm=512, tn=512)
        kv = _norm_matmul(memf, row(g_mem, l), w_kv_b, l, tm=512, tn=512, out_dtype=BF16)
        o = _xattn(xf, row(g_pre_xa, l), w_q_b, kv, l, seq=seq, tm=512)
        xf = _proj_norm_res([o], w_o_b, xf, row(g_post_xa, l), l, tm=512, tn=512)
        xf = _mlp(xf, row(g_pre_mlp, l), w_up_b, w_down_b, row(g_post_mlp, l), l,
                  tm=512, tf=512)
    return xf.reshape(batch, seq, d)
```

```python
import functools

import jax
import jax.numpy as jnp
from jax import lax
from jax.experimental import pallas as pl
from jax.experimental.pallas import tpu as pltpu

F32 = jnp.float32
BF16 = jnp.bfloat16

EPS = 1e-6
H_A = 8
H_B = 8
CONV_K = 31
CHUNK = 128
N_MEM = 256
XA_HEADS = 4

V7X_VMEM_BYTES = 64 * 1024 * 1024
LANES = 128
SUBLANES = 8
BF16_SUBLANES = 16
CONV_PAD = 32
CONV_ROWS = 64
CONV_TAIL = CONV_ROWS + CONV_PAD
NORM_ROWS = 32
NORM_COLS = 512


def _vmem_limit(nbytes):
    return int(min(nbytes + 12 * 1024 * 1024, V7X_VMEM_BYTES - 4 * 1024 * 1024))


def _lane_fold(v):
    acc = v[:, 0:LANES]
    for t in range(1, v.shape[1] // LANES):
        acc = acc + v[:, t * LANES:(t + 1) * LANES]
    return acc


def _row_rstd(src_ref, rows, d):
    part = None
    for s in range(d // NORM_COLS):
        a = src_ref[rows, s * NORM_COLS:(s + 1) * NORM_COLS]
        f = _lane_fold(a * a)
        part = f if part is None else part + f
    return lax.rsqrt(jnp.sum(part, axis=-1, keepdims=True) * (1.0 / d) + EPS)


def _scale_rows_to(src_ref, rows, rstd, g_ref, h_ref):
    d = src_ref.shape[1]
    for s in range(d // NORM_COLS):
        cols = slice(s * NORM_COLS, (s + 1) * NORM_COLS)
        h_ref[rows, cols] = (src_ref[rows, cols] * rstd * g_ref[:, cols]).astype(h_ref.dtype)


def _rmsnorm_rows_to(x_ref, g_ref, h_ref):
    tm, d = x_ref.shape

    def body(r, carry):
        rows = pl.ds(pl.multiple_of(r * NORM_ROWS, NORM_ROWS), NORM_ROWS)
        _scale_rows_to(x_ref, rows, _row_rstd(x_ref, rows, d), g_ref, h_ref)
        return carry

    lax.fori_loop(0, tm // NORM_ROWS, body, 0)


def _post_norm_residual(o_ref, x_ref, g_ref, gn_ref, h_ref):
    tm, d = o_ref.shape

    def body(r, carry):
        rows = pl.ds(pl.multiple_of(r * NORM_ROWS, NORM_ROWS), NORM_ROWS)
        rstd = _row_rstd(o_ref, rows, d)
        part = None
        for s in range(d // NORM_COLS):
            cols = slice(s * NORM_COLS, (s + 1) * NORM_COLS)
            y = x_ref[rows, cols] + o_ref[rows, cols] * rstd * g_ref[:, cols]
            o_ref[rows, cols] = y
            if h_ref is not None:
                f = _lane_fold(y * y)
                part = f if part is None else part + f
        if h_ref is not None:
            rstd_n = lax.rsqrt(jnp.sum(part, axis=-1, keepdims=True) * (1.0 / d) + EPS)
            _scale_rows_to(o_ref, rows, rstd_n, gn_ref, h_ref)
        return carry

    lax.fori_loop(0, tm // NORM_ROWS, body, 0)


def _group_layernorm(v, g, b):
    mu = jnp.mean(v, axis=-1, keepdims=True)
    xc = v - mu
    var = jnp.mean(xc * xc, axis=-1, keepdims=True)
    return xc * lax.rsqrt(var + EPS) * g + b


def _prenorm_kernel(x_ref, g_ref, h_ref):
    _rmsnorm_rows_to(x_ref, g_ref, h_ref)


def _prenorm(x, g, *, tm):
    m, d = x.shape
    return pl.pallas_call(
        _prenorm_kernel,
        grid=(m // tm,),
        in_specs=[pl.BlockSpec((tm, d), lambda i: (i, 0)),
                  pl.BlockSpec((1, d), lambda i: (0, 0))],
        out_specs=pl.BlockSpec((tm, d), lambda i: (i, 0)),
        out_shape=jax.ShapeDtypeStruct((m, d), BF16),
        compiler_params=pltpu.CompilerParams(
            dimension_semantics=("arbitrary",),
            vmem_limit_bytes=_vmem_limit(2 * tm * d * 6)),
        name="prenorm",
    )(x, g)


def _norm_matmul_kernel(x_ref, g_ref, w_ref, o_ref, h_ref):
    @pl.when(pl.program_id(1) == 0)
    def _():
        _rmsnorm_rows_to(x_ref, g_ref, h_ref)

    o_ref[...] = jnp.dot(h_ref[...], w_ref[...],
                         preferred_element_type=F32).astype(o_ref.dtype)


def _norm_matmul(x, g, w, layer, *, tm, tn, out_dtype):
    m, d = x.shape
    n = w.shape[2]
    nbytes = 2 * tm * d * 4 + tm * d * 2 + 2 * d * tn * 2 + 2 * tm * tn * 4
    return pl.pallas_call(
        _norm_matmul_kernel,
        grid=(m // tm, n // tn),
        in_specs=[
            pl.BlockSpec((tm, d), lambda i, j: (i, 0)),
            pl.BlockSpec((1, d), lambda i, j: (0, 0)),
            pl.BlockSpec((None, d, tn), lambda i, j: (layer, 0, j)),
        ],
        out_specs=pl.BlockSpec((tm, tn), lambda i, j: (i, j)),
        out_shape=jax.ShapeDtypeStruct((m, n), out_dtype),
        scratch_shapes=[pltpu.VMEM((tm, d), BF16)],
        compiler_params=pltpu.CompilerParams(
            dimension_semantics=("arbitrary", "arbitrary"),
            vmem_limit_bytes=_vmem_limit(nbytes)),
        name="norm_matmul",
    )(x, g, w)


def _inproj_glu_kernel(h_ref, wa_ref, wg_ref, o_ref):
    h = h_ref[...]
    a = jnp.dot(h, wa_ref[...], preferred_element_type=F32)
    gate = jnp.dot(h, wg_ref[...], preferred_element_type=F32)
    o_ref[...] = a * jax.nn.sigmoid(gate)


def _inproj_glu(h, w_in, layer, *, w_a, tm, tn):
    m, d = h.shape
    gate_off = w_a // tn
    nbytes = 2 * tm * d * 2 + 4 * d * tn * 2 + 2 * tm * tn * 4 + 2 * tm * tn * 4
    return pl.pallas_call(
        _inproj_glu_kernel,
        grid=(m // tm, w_a // tn),
        in_specs=[
            pl.BlockSpec((tm, d), lambda i, j: (i, 0)),
            pl.BlockSpec((None, d, tn), lambda i, j: (layer, 0, j)),
            pl.BlockSpec((None, d, tn), lambda i, j: (layer, 0, j + gate_off)),
        ],
        out_specs=pl.BlockSpec((tm, tn), lambda i, j: (i, j)),
        out_shape=jax.ShapeDtypeStruct((m, w_a), F32),
        compiler_params=pltpu.CompilerParams(
            dimension_semantics=("arbitrary", "arbitrary"),
            vmem_limit_bytes=_vmem_limit(nbytes)),
        name="inproj_glu",
    )(h, w_in, w_in)


def _inproj_gmlp_kernel(h_ref, wu_ref, wv_ref, lg_ref, lb_ref, ws_ref, bs_ref, o_ref):
    h = h_ref[...]
    u = jax.nn.gelu(jnp.dot(h, wu_ref[...], preferred_element_type=F32))
    v = jax.nn.gelu(jnp.dot(h, wv_ref[...], preferred_element_type=F32))
    vn = _group_layernorm(v, lg_ref[...], lb_ref[...]).astype(BF16)

    t_idx = lax.broadcasted_iota(jnp.int32, (CHUNK, CHUNK), 0)
    s_idx = lax.broadcasted_iota(jnp.int32, (CHUNK, CHUNK), 1)
    w_sp = jnp.where(t_idx >= s_idx, ws_ref[...], 0.0).astype(BF16)
    bias = bs_ref[...]
    for c in range(o_ref.shape[0] // CHUNK):
        rows = slice(c * CHUNK, (c + 1) * CHUNK)
        s = jnp.dot(w_sp, vn[rows, :], preferred_element_type=F32)
        o_ref[rows, :] = (u[rows, :] * (s + bias)).astype(o_ref.dtype)


def _inproj_gmlp(h, w_in, ln_g, ln_b, w_sp, b_sp, layer, *, w_a, w_b, tm):
    m, d = h.shape
    d_b = w_b // H_B
    u_off = 2 * w_a // d_b
    v_off = u_off + H_B
    nbytes = 2 * tm * d * 2 + 4 * d * d_b * 2 + 2 * tm * d_b * 2 + 4 * tm * d_b * 4
    return pl.pallas_call(
        _inproj_gmlp_kernel,
        grid=(m // tm, H_B),
        in_specs=[
            pl.BlockSpec((tm, d), lambda i, j: (i, 0)),
            pl.BlockSpec((None, d, d_b), lambda i, j: (layer, 0, j + u_off)),
            pl.BlockSpec((None, d, d_b), lambda i, j: (layer, 0, j + v_off)),
            pl.BlockSpec((1, d_b), lambda i, j: (0, j)),
            pl.BlockSpec((1, d_b), lambda i, j: (0, j)),
            pl.BlockSpec((None, None, CHUNK, CHUNK), lambda i, j: (layer, j, 0, 0)),
            pl.BlockSpec((None, CHUNK, 1), lambda i, j: (j, 0, 0)),
        ],
        out_specs=pl.BlockSpec((tm, d_b), lambda i, j: (i, j)),
        out_shape=jax.ShapeDtypeStruct((m, w_b), BF16),
        compiler_params=pltpu.CompilerParams(
            dimension_semantics=("arbitrary", "arbitrary"),
            vmem_limit_bytes=_vmem_limit(nbytes)),
        name="inproj_gmlp",
    )(h, w_in, w_in, ln_g, ln_b, w_sp, b_sp)


def _conv_ln_silu_kernel(x_ref, w_ref, cb_ref, lg_ref, lb_ref, o_ref, p_ref):
    seq, c = x_ref.shape
    rows = CONV_ROWS
    p_ref[0, 0:CONV_PAD, :] = jnp.zeros((CONV_PAD, c), F32)
    p_ref[0, CONV_PAD + seq:, :] = jnp.zeros((CONV_TAIL, c), F32)
    p_ref[0, CONV_PAD:CONV_PAD + seq, :] = x_ref[...]

    def build(i, carry):
        t0 = pl.multiple_of(i * rows, rows)
        win = p_ref[0, pl.ds(t0, rows + SUBLANES), :]
        for r in range(1, SUBLANES):
            p_ref[r, pl.ds(t0, rows), :] = win[r:r + rows, :]
        return carry

    lax.fori_loop(0, (seq + rows) // rows, build, 0)
    shift = CONV_PAD - (CONV_K - 1)

    def body(i, carry):
        t0 = pl.multiple_of(i * rows, rows)
        acc = jnp.broadcast_to(cb_ref[...], (rows, c))
        for k in range(CONV_K):
            q, r = divmod(shift + k, SUBLANES)
            start = pl.multiple_of(t0 + SUBLANES * q, SUBLANES)
            acc = acc + w_ref[k:k + 1, :] * p_ref[r, pl.ds(start, rows), :]
        y = _group_layernorm(acc, lg_ref[...], lb_ref[...])
        o_ref[pl.ds(t0, rows), :] = (y * jax.nn.sigmoid(y)).astype(o_ref.dtype)
        return carry

    lax.fori_loop(0, seq // rows, body, 0)


def _conv_ln_silu(hglu, conv_w, conv_b, ln_g, ln_b, *, batch, seq):
    m, w_a = hglu.shape
    cg = w_a // H_A
    return pl.pallas_call(
        _conv_ln_silu_kernel,
        grid=(batch, H_A),
        in_specs=[
            pl.BlockSpec((seq, cg), lambda b, c: (b, c)),
            pl.BlockSpec((CONV_K, cg), lambda b, c: (0, c)),
            pl.BlockSpec((1, cg), lambda b, c: (0, c)),
            pl.BlockSpec((1, cg), lambda b, c: (0, c)),
            pl.BlockSpec((1, cg), lambda b, c: (0, c)),
        ],
        out_specs=pl.BlockSpec((seq, cg), lambda b, c: (b, c)),
        out_shape=jax.ShapeDtypeStruct((m, w_a), BF16),
        scratch_shapes=[pltpu.VMEM((SUBLANES, CONV_PAD + seq + CONV_TAIL, cg), F32)],
        compiler_params=pltpu.CompilerParams(
            dimension_semantics=("arbitrary", "arbitrary"),
            vmem_limit_bytes=_vmem_limit(
                SUBLANES * (CONV_PAD + seq + CONV_TAIL) * cg * 4 + 2 * seq * cg * 6)),
        name="conv_ln_silu",
    )(hglu, conv_w, conv_b, ln_g, ln_b)


def _proj_norm_res_kernel(*refs, n_a):
    a_refs = refs[:n_a]
    w_ref, x_ref, g_ref, gn_ref, o_ref, h_ref = refs[n_a:]
    j = pl.program_id(1)
    tn = w_ref.shape[1]
    acc = None
    off = 0
    for a_ref in a_refs:
        ka = a_ref.shape[1]
        part = jnp.dot(a_ref[...], w_ref[off:off + ka, :], preferred_element_type=F32)
        acc = part if acc is None else acc + part
        off += ka
    o_ref[:, pl.ds(pl.multiple_of(j * tn, tn), tn)] = acc

    @pl.when(j == pl.num_programs(1) - 1)
    def _():
        _post_norm_residual(o_ref, x_ref, g_ref, gn_ref, h_ref)


def _proj_norm_res(a_list, w, x, g, g_next, layer, *, tm, tn):
    m, d = x.shape
    k = w.shape[1]
    n_a = len(a_list)
    nbytes = 2 * tm * k * 2 + 2 * k * tn * 2 + tm * d * 4 + 2 * tm * d * 4 + 2 * tm * d * 2
    a_specs = [pl.BlockSpec((tm, a.shape[1]), lambda i, j: (i, 0)) for a in a_list]
    return pl.pallas_call(
        functools.partial(_proj_norm_res_kernel, n_a=n_a),
        grid=(m // tm, d // tn),
        in_specs=a_specs + [
            pl.BlockSpec((None, k, tn), lambda i, j: (layer, 0, j)),
            pl.BlockSpec((tm, d), lambda i, j: (i, 0), pipeline_mode=pl.Buffered(1)),
            pl.BlockSpec((1, d), lambda i, j: (0, 0)),
            pl.BlockSpec((1, d), lambda i, j: (0, 0)),
        ],
        out_specs=[pl.BlockSpec((tm, d), lambda i, j: (i, 0)),
                   pl.BlockSpec((tm, d), lambda i, j: (i, 0))],
        out_shape=[jax.ShapeDtypeStruct((m, d), F32), jax.ShapeDtypeStruct((m, d), BF16)],
        compiler_params=pltpu.CompilerParams(
            dimension_semantics=("arbitrary", "arbitrary"),
            vmem_limit_bytes=_vmem_limit(nbytes)),
        name="proj_norm_res",
    )(*a_list, w, x, g, g_next)


def _xattn_kernel(h_ref, wq_ref, k_ref, v_ref, o_ref, *, scale):
    q = jnp.dot(h_ref[...], wq_ref[...], preferred_element_type=F32).astype(BF16)
    s = lax.dot_general(q, k_ref[...], (((1,), (1,)), ((), ())),
                        preferred_element_type=F32) * scale
    p = jnp.exp(s - jnp.max(s, axis=-1, keepdims=True))
    p = (p / jnp.sum(p, axis=-1, keepdims=True)).astype(BF16)
    o_ref[...] = jnp.dot(p, v_ref[...], preferred_element_type=F32).astype(o_ref.dtype)


def _xattn(h, w_q, kv, layer, *, seq, tm):
    m, d = h.shape
    dh = d // XA_HEADS
    nbytes = (2 * tm * d * 2 + 2 * d * dh * 2 + 8 * N_MEM * dh * 2 + 2 * tm * dh * 2
              + tm * dh * 6)
    return pl.pallas_call(
        functools.partial(_xattn_kernel, scale=float(dh) ** -0.5),
        grid=(m // tm, XA_HEADS),
        in_specs=[
            pl.BlockSpec((tm, d), lambda i, j: (i, 0)),
            pl.BlockSpec((None, d, dh), lambda i, j: (layer, 0, j)),
            pl.BlockSpec((N_MEM, dh), lambda i, j: ((i * tm) // seq, j)),
            pl.BlockSpec((N_MEM, dh), lambda i, j: ((i * tm) // seq, j + XA_HEADS)),
        ],
        out_specs=pl.BlockSpec((tm, dh), lambda i, j: (i, j)),
        out_shape=jax.ShapeDtypeStruct((m, d), BF16),
        compiler_params=pltpu.CompilerParams(
            dimension_semantics=("arbitrary", "arbitrary"),
            vmem_limit_bytes=_vmem_limit(nbytes)),
        name="xattn",
    )(h, w_q, kv, kv)


def _mlp_kernel(*refs, emit_h):
    if emit_h:
        h_ref, x_ref, wu_ref, wd_ref, g_ref, gn_ref, o_ref, hn_ref = refs
    else:
        h_ref, x_ref, wu_ref, wd_ref, g_ref, o_ref = refs
        gn_ref = hn_ref = None
    f = pl.program_id(1)

    @pl.when(f == 0)
    def _():
        o_ref[...] = jnp.zeros(o_ref.shape, F32)

    u = jnp.dot(h_ref[...], wu_ref[...], preferred_element_type=F32)
    u = jnp.square(jnp.maximum(u, 0.0)).astype(BF16)
    o_ref[...] += jnp.dot(u, wd_ref[...], preferred_element_type=F32)

    @pl.when(f == pl.num_programs(1) - 1)
    def _():
        _post_norm_residual(o_ref, x_ref, g_ref, gn_ref, hn_ref)


def _mlp(h, x, w_up, w_down, g_post, g_next, layer, *, tm, tf):
    m, d = x.shape
    d_ff = w_up.shape[2]
    emit_h = g_next is not None
    nbytes = (tm * d * 2 + tm * d * 4 + 4 * d * tf * 2 + 2 * tm * d * 4
              + (2 * tm * d * 2 if emit_h else 0))
    row_once = dict(pipeline_mode=pl.Buffered(1))
    in_specs = [
        pl.BlockSpec((tm, d), lambda i, f: (i, 0), **row_once),
        pl.BlockSpec((tm, d), lambda i, f: (i, 0), **row_once),
        pl.BlockSpec((None, d, tf), lambda i, f: (layer, 0, f)),
        pl.BlockSpec((None, tf, d), lambda i, f: (layer, f, 0)),
        pl.BlockSpec((1, d), lambda i, f: (0, 0)),
    ]
    out_specs = [pl.BlockSpec((tm, d), lambda i, f: (i, 0))]
    out_shape = [jax.ShapeDtypeStruct((m, d), F32)]
    args = [h, x, w_up, w_down, g_post]
    if emit_h:
        in_specs.append(pl.BlockSpec((1, d), lambda i, f: (0, 0)))
        out_specs.append(pl.BlockSpec((tm, d), lambda i, f: (i, 0)))
        out_shape.append(jax.ShapeDtypeStruct((m, d), BF16))
        args.append(g_next)
    outs = pl.pallas_call(
        functools.partial(_mlp_kernel, emit_h=emit_h),
        grid=(m // tm, d_ff // tf),
        in_specs=in_specs,
        out_specs=out_specs,
        out_shape=out_shape,
        compiler_params=pltpu.CompilerParams(
            dimension_semantics=("arbitrary", "arbitrary"),
            vmem_limit_bytes=_vmem_limit(nbytes)),
        name="mlp",
    )(*args)
    return (outs[0], outs[1]) if emit_h else (outs[0], None)


def kernel(x, mem, g_pre_mix, w_in, conv_w, conv_b, ln_a_g, ln_a_b, ln_v_g, ln_v_b, w_spatial, b_spatial, w_out, g_post_mix, g_pre_xa, g_mem, w_q, w_kv, w_o, g_post_xa, g_pre_mlp, w_up, w_down, g_post_mlp):
    batch, seq, d = x.shape
    depth = w_in.shape[0]
    w_a = conv_w.shape[2]
    w_b = ln_v_g.shape[1]
    m = batch * seq

    w_in_b = w_in.astype(BF16)
    w_out_b = w_out.astype(BF16)
    w_q_b = w_q.astype(BF16)
    w_kv_b = w_kv.astype(BF16)
    w_o_b = w_o.astype(BF16)
    w_up_b = w_up.astype(BF16)
    w_down_b = w_down.astype(BF16)

    xf = x.reshape(m, d)
    memf = mem.reshape(batch * N_MEM, d)

    def row(p, l):
        return p[l].reshape(1, -1)

    h = _prenorm(xf, row(g_pre_mix, 0), tm=512)
    for l in range(depth):
        hglu = _inproj_glu(h, w_in_b, l, w_a=w_a, tm=1024, tn=512)
        y_a = _conv_ln_silu(hglu, conv_w[l], row(conv_b, l), row(ln_a_g, l), row(ln_a_b, l),
                            batch=batch, seq=seq)
        y_b = _inproj_gmlp(h, w_in_b, row(ln_v_g, l), row(ln_v_b, l), w_spatial,
                           b_spatial[l].reshape(H_B, CHUNK, 1), l, w_a=w_a, w_b=w_b, tm=1024)
        xf, h = _proj_norm_res([y_a, y_b], w_out_b, xf, row(g_post_mix, l),
                               row(g_pre_xa, l), l, tm=512, tn=512)
        kv = _norm_matmul(memf, row(g_mem, l), w_kv_b, l, tm=512, tn=512, out_dtype=BF16)
        o = _xattn(h, w_q_b, kv, l, seq=seq, tm=1024)
        xf, h = _proj_norm_res([o], w_o_b, xf, row(g_post_xa, l), row(g_pre_mlp, l), l,
                               tm=512, tn=512)
        g_next = row(g_pre_mix, l + 1) if l + 1 < depth else None
        xf, h = _mlp(h, xf, w_up_b, w_down_b, row(g_post_mlp, l), g_next, l, tm=512, tf=512)
    return xf.reshape(batch, seq, d)
```

```python
import functools

import jax
import jax.numpy as jnp
from jax import lax
from jax.experimental import pallas as pl
from jax.experimental.pallas import tpu as pltpu

F32 = jnp.float32
BF16 = jnp.bfloat16

EPS = 1e-6
H_A = 8
H_B = 8
CONV_K = 31
CHUNK = 128
N_MEM = 256
XA_HEADS = 4

V7X_VMEM_BYTES = 64 * 1024 * 1024
LANES = 128
SUBLANES = 8
CONV_PAD = 32
CONV_ROWS = 64
CONV_TAIL = CONV_ROWS + CONV_PAD
CONV_LN_ROWS = 256
NORM_ROWS = 32
NORM_COLS = 512


def _vmem_limit(nbytes):
    return int(min(nbytes + 12 * 1024 * 1024, V7X_VMEM_BYTES - 4 * 1024 * 1024))


def _lane_fold(v):
    acc = v[:, 0:LANES]
    for t in range(1, v.shape[1] // LANES):
        acc = acc + v[:, t * LANES:(t + 1) * LANES]
    return acc


def _row_rstd(src_ref, rows, d):
    part = None
    for s in range(d // NORM_COLS):
        a = src_ref[rows, s * NORM_COLS:(s + 1) * NORM_COLS]
        f = _lane_fold(a * a)
        part = f if part is None else part + f
    return lax.rsqrt(jnp.sum(part, axis=-1, keepdims=True) * (1.0 / d) + EPS)


def _scale_rows_to(src_ref, rows, rstd, g_ref, h_ref):
    d = src_ref.shape[1]
    for s in range(d // NORM_COLS):
        cols = slice(s * NORM_COLS, (s + 1) * NORM_COLS)
        h_ref[rows, cols] = (src_ref[rows, cols] * rstd * g_ref[:, cols]).astype(h_ref.dtype)


def _rmsnorm_rows_to(x_ref, g_ref, h_ref):
    tm, d = x_ref.shape

    def body(r, carry):
        rows = pl.ds(pl.multiple_of(r * NORM_ROWS, NORM_ROWS), NORM_ROWS)
        _scale_rows_to(x_ref, rows, _row_rstd(x_ref, rows, d), g_ref, h_ref)
        return carry

    lax.fori_loop(0, tm // NORM_ROWS, body, 0)


def _post_norm_residual_to_hbm(o_scr, x_scr, g_ref, gn_ref, h_scr, o_hbm, h_hbm, row0, sems):
    tm, d = o_scr.shape
    n = tm // NORM_ROWS

    def copies(r):
        rows = pl.ds(pl.multiple_of(r * NORM_ROWS, NORM_ROWS), NORM_ROWS)
        dst = pl.ds(pl.multiple_of(row0 + r * NORM_ROWS, NORM_ROWS), NORM_ROWS)
        out = [pltpu.make_async_copy(o_scr.at[rows, :], o_hbm.at[dst, :], sems.at[0])]
        if h_scr is not None:
            out.append(pltpu.make_async_copy(h_scr.at[rows, :], h_hbm.at[dst, :], sems.at[1]))
        return out

    def body(r, carry):
        rows = pl.ds(pl.multiple_of(r * NORM_ROWS, NORM_ROWS), NORM_ROWS)
        rstd = _row_rstd(o_scr, rows, d)
        part = None
        for s in range(d // NORM_COLS):
            cols = slice(s * NORM_COLS, (s + 1) * NORM_COLS)
            y = x_scr[rows, cols] + o_scr[rows, cols] * rstd * g_ref[:, cols]
            o_scr[rows, cols] = y
            if h_scr is not None:
                f = _lane_fold(y * y)
                part = f if part is None else part + f
        if h_scr is not None:
            rstd_n = lax.rsqrt(jnp.sum(part, axis=-1, keepdims=True) * (1.0 / d) + EPS)
            _scale_rows_to(o_scr, rows, rstd_n, gn_ref, h_scr)
        for c in copies(r):
            c.start()
        return carry

    lax.fori_loop(0, n, body, 0)

    def drain(r, carry):
        for c in copies(r):
            c.wait()
        return carry

    lax.fori_loop(0, n, drain, 0)


def _group_layernorm(v, g, b):
    mu = jnp.mean(v, axis=-1, keepdims=True)
    xc = v - mu
    var = jnp.mean(xc * xc, axis=-1, keepdims=True)
    return xc * lax.rsqrt(var + EPS) * g + b


def _prenorm_kernel(x_ref, g_ref, h_ref):
    _rmsnorm_rows_to(x_ref, g_ref, h_ref)


def _prenorm(x, g, *, tm):
    m, d = x.shape
    return pl.pallas_call(
        _prenorm_kernel,
        grid=(m // tm,),
        in_specs=[pl.BlockSpec((tm, d), lambda i: (i, 0)),
                  pl.BlockSpec((1, d), lambda i: (0, 0))],
        out_specs=pl.BlockSpec((tm, d), lambda i: (i, 0)),
        out_shape=jax.ShapeDtypeStruct((m, d), BF16),
        compiler_params=pltpu.CompilerParams(
            dimension_semantics=("arbitrary",),
            vmem_limit_bytes=_vmem_limit(2 * tm * d * 6)),
        name="prenorm",
    )(x, g)


def _norm_matmul_kernel(x_ref, g_ref, w_ref, o_ref, h_ref):
    @pl.when(pl.program_id(1) == 0)
    def _():
        _rmsnorm_rows_to(x_ref, g_ref, h_ref)

    o_ref[...] = jnp.dot(h_ref[...], w_ref[...].astype(BF16),
                         preferred_element_type=F32).astype(o_ref.dtype)


def _norm_matmul(x, g, w, layer, *, tm, tn, out_dtype):
    m, d = x.shape
    n = w.shape[2]
    nbytes = tm * d * 4 + tm * d * 2 + 2 * d * tn * 4 + d * tn * 2 + 2 * tm * tn * 4
    return pl.pallas_call(
        _norm_matmul_kernel,
        grid=(m // tm, n // tn),
        in_specs=[
            pl.BlockSpec((tm, d), lambda i, j: (i, 0), pipeline_mode=pl.Buffered(1)),
            pl.BlockSpec((1, d), lambda i, j: (0, 0)),
            pl.BlockSpec((None, d, tn), lambda i, j: (layer, 0, j)),
        ],
        out_specs=pl.BlockSpec((tm, tn), lambda i, j: (i, j)),
        out_shape=jax.ShapeDtypeStruct((m, n), out_dtype),
        scratch_shapes=[pltpu.VMEM((tm, d), BF16)],
        compiler_params=pltpu.CompilerParams(
            dimension_semantics=("arbitrary", "arbitrary"),
            vmem_limit_bytes=_vmem_limit(nbytes)),
        name="norm_matmul",
    )(x, g, w)


_W_ONCE = dict(pipeline_mode=pl.Buffered(1))


def _inproj_glu_kernel(h_ref, wa_ref, wg_ref, o_ref, wab_ref, wgb_ref):
    @pl.when(pl.program_id(1) == 0)
    def _():
        wab_ref[...] = wa_ref[...].astype(BF16)
        wgb_ref[...] = wg_ref[...].astype(BF16)

    h = h_ref[...]
    a = jnp.dot(h, wab_ref[...], preferred_element_type=F32)
    gate = jnp.dot(h, wgb_ref[...], preferred_element_type=F32)
    o_ref[...] = a * jax.nn.sigmoid(gate)


def _inproj_glu(h, w_in, layer, *, w_a, tm, tn):
    m, d = h.shape
    gate_off = w_a // tn
    nbytes = 2 * tm * d * 2 + 2 * d * tn * 4 + 2 * d * tn * 2 + 4 * tm * tn * 4
    return pl.pallas_call(
        _inproj_glu_kernel,
        grid=(w_a // tn, m // tm),
        in_specs=[
            pl.BlockSpec((tm, d), lambda j, i: (i, 0)),
            pl.BlockSpec((None, d, tn), lambda j, i: (layer, 0, j), **_W_ONCE),
            pl.BlockSpec((None, d, tn), lambda j, i: (layer, 0, j + gate_off), **_W_ONCE),
        ],
        out_specs=pl.BlockSpec((tm, tn), lambda j, i: (i, j)),
        out_shape=jax.ShapeDtypeStruct((m, w_a), F32),
        scratch_shapes=[pltpu.VMEM((d, tn), BF16), pltpu.VMEM((d, tn), BF16)],
        compiler_params=pltpu.CompilerParams(
            dimension_semantics=("arbitrary", "arbitrary"),
            vmem_limit_bytes=_vmem_limit(nbytes)),
        name="inproj_glu",
    )(h, w_in, w_in)


def _inproj_gmlp_kernel(h_ref, wu_ref, wv_ref, lg_ref, lb_ref, ws_ref, bs_ref, o_ref,
                        wub_ref, wvb_ref):
    @pl.when(pl.program_id(1) == 0)
    def _():
        wub_ref[...] = wu_ref[...].astype(BF16)
        wvb_ref[...] = wv_ref[...].astype(BF16)

    h = h_ref[...]
    u = jax.nn.gelu(jnp.dot(h, wub_ref[...], preferred_element_type=F32))
    v = jax.nn.gelu(jnp.dot(h, wvb_ref[...], preferred_element_type=F32))
    vn = _group_layernorm(v, lg_ref[...], lb_ref[...]).astype(BF16)

    t_idx = lax.broadcasted_iota(jnp.int32, (CHUNK, CHUNK), 0)
    s_idx = lax.broadcasted_iota(jnp.int32, (CHUNK, CHUNK), 1)
    w_sp = jnp.where(t_idx >= s_idx, ws_ref[...], 0.0).astype(BF16)
    bias = bs_ref[...]
    for c in range(o_ref.shape[0] // CHUNK):
        rows = slice(c * CHUNK, (c + 1) * CHUNK)
        s = jnp.dot(w_sp, vn[rows, :], preferred_element_type=F32)
        o_ref[rows, :] = (u[rows, :] * (s + bias)).astype(o_ref.dtype)


def _inproj_gmlp(h, w_in, ln_g, ln_b, w_sp, b_sp, layer, *, w_a, w_b, tm):
    m, d = h.shape
    d_b = w_b // H_B
    u_off = 2 * w_a // d_b
    v_off = u_off + H_B
    nbytes = (2 * tm * d * 2 + 2 * d * d_b * 4 + 2 * d * d_b * 2 + 2 * tm * d_b * 2
              + 4 * tm * d_b * 4)
    return pl.pallas_call(
        _inproj_gmlp_kernel,
        grid=(H_B, m // tm),
        in_specs=[
            pl.BlockSpec((tm, d), lambda j, i: (i, 0)),
            pl.BlockSpec((None, d, d_b), lambda j, i: (layer, 0, j + u_off), **_W_ONCE),
            pl.BlockSpec((None, d, d_b), lambda j, i: (layer, 0, j + v_off), **_W_ONCE),
            pl.BlockSpec((1, d_b), lambda j, i: (0, j)),
            pl.BlockSpec((1, d_b), lambda j, i: (0, j)),
            pl.BlockSpec((None, None, CHUNK, CHUNK), lambda j, i: (layer, j, 0, 0)),
            pl.BlockSpec((None, CHUNK, 1), lambda j, i: (j, 0, 0)),
        ],
        out_specs=pl.BlockSpec((tm, d_b), lambda j, i: (i, j)),
        out_shape=jax.ShapeDtypeStruct((m, w_b), BF16),
        scratch_shapes=[pltpu.VMEM((d, d_b), BF16), pltpu.VMEM((d, d_b), BF16)],
        compiler_params=pltpu.CompilerParams(
            dimension_semantics=("arbitrary", "arbitrary"),
            vmem_limit_bytes=_vmem_limit(nbytes)),
        name="inproj_gmlp",
    )(h, w_in, w_in, ln_g, ln_b, w_sp, b_sp)


def _conv_ln_silu_kernel(x_ref, w_ref, cb_ref, lg_ref, lb_ref, o_ref, p_ref, a_ref):
    seq, c = x_ref.shape
    rows = CONV_ROWS
    p_ref[0, 0:CONV_PAD, :] = jnp.zeros((CONV_PAD, c), F32)
    p_ref[0, CONV_PAD + seq:, :] = jnp.zeros((CONV_TAIL, c), F32)
    p_ref[0, CONV_PAD:CONV_PAD + seq, :] = x_ref[...]

    def build(i, carry):
        t0 = pl.multiple_of(i * rows, rows)
        win = p_ref[0, pl.ds(t0, rows + SUBLANES), :]
        for r in range(1, SUBLANES):
            p_ref[r, pl.ds(t0, rows), :] = win[r:r + rows, :]
        return carry

    lax.fori_loop(0, (seq + rows) // rows, build, 0)
    shift = CONV_PAD - (CONV_K - 1)

    def taps(i, carry):
        t0 = pl.multiple_of(i * rows, rows)
        acc = jnp.broadcast_to(cb_ref[...], (rows, c))
        for k in range(CONV_K):
            q, r = divmod(shift + k, SUBLANES)
            start = pl.multiple_of(t0 + SUBLANES * q, SUBLANES)
            acc = acc + w_ref[k:k + 1, :] * p_ref[r, pl.ds(start, rows), :]
        a_ref[pl.ds(t0, rows), :] = acc
        return carry

    lax.fori_loop(0, seq // rows, taps, 0)

    def finish(i, carry):
        t0 = pl.multiple_of(i * CONV_LN_ROWS, CONV_LN_ROWS)
        y = _group_layernorm(a_ref[pl.ds(t0, CONV_LN_ROWS), :], lg_ref[...], lb_ref[...])
        o_ref[pl.ds(t0, CONV_LN_ROWS), :] = (y * jax.nn.sigmoid(y)).astype(o_ref.dtype)
        return carry

    lax.fori_loop(0, seq // CONV_LN_ROWS, finish, 0)


def _conv_ln_silu(hglu, conv_w, conv_b, ln_g, ln_b, *, batch, seq):
    m, w_a = hglu.shape
    cg = w_a // H_A
    p_rows = CONV_PAD + seq + CONV_TAIL
    return pl.pallas_call(
        _conv_ln_silu_kernel,
        grid=(batch, H_A),
        in_specs=[
            pl.BlockSpec((seq, cg), lambda b, c: (b, c)),
            pl.BlockSpec((CONV_K, cg), lambda b, c: (0, c)),
            pl.BlockSpec((1, cg), lambda b, c: (0, c)),
            pl.BlockSpec((1, cg), lambda b, c: (0, c)),
            pl.BlockSpec((1, cg), lambda b, c: (0, c)),
        ],
        out_specs=pl.BlockSpec((seq, cg), lambda b, c: (b, c)),
        out_shape=jax.ShapeDtypeStruct((m, w_a), BF16),
        scratch_shapes=[pltpu.VMEM((SUBLANES, p_rows, cg), F32), pltpu.VMEM((seq, cg), F32)],
        compiler_params=pltpu.CompilerParams(
            dimension_semantics=("arbitrary", "arbitrary"),
            vmem_limit_bytes=_vmem_limit(
                SUBLANES * p_rows * cg * 4 + seq * cg * 4 + 2 * seq * cg * 6)),
        name="conv_ln_silu",
    )(hglu, conv_w, conv_b, ln_g, ln_b)


def _proj_norm_res_kernel(*refs, n_a):
    a_refs = refs[:n_a]
    (w_ref, x_ref, g_ref, gn_ref, o_hbm, h_hbm, o_scr, x_scr, h_scr, sems) = refs[n_a:]
    i = pl.program_id(0)
    j = pl.program_id(1)
    tm, tn = x_ref.shape
    acc = None
    off = 0
    for a_ref in a_refs:
        ka = a_ref.shape[1]
        part = jnp.dot(a_ref[...], w_ref[off:off + ka, :], preferred_element_type=F32)
        acc = part if acc is None else acc + part
        off += ka
    cols = pl.ds(pl.multiple_of(j * tn, tn), tn)
    o_scr[:, cols] = acc
    x_scr[:, cols] = x_ref[...]

    @pl.when(j == pl.num_programs(1) - 1)
    def _():
        _post_norm_residual_to_hbm(o_scr, x_scr, g_ref, gn_ref, h_scr, o_hbm, h_hbm,
                                   i * tm, sems)


def _proj_norm_res(a_list, w, x, g, g_next, layer, *, tm, tn):
    m, d = x.shape
    k = w.shape[1]
    n_a = len(a_list)
    nbytes = 2 * tm * k * 2 + 2 * k * tn * 2 + 2 * tm * tn * 4 + 2 * tm * d * 4 + tm * d * 2
    a_specs = [pl.BlockSpec((tm, a.shape[1]), lambda i, j: (i, 0)) for a in a_list]
    return pl.pallas_call(
        functools.partial(_proj_norm_res_kernel, n_a=n_a),
        grid=(m // tm, d // tn),
        in_specs=a_specs + [
            pl.BlockSpec((None, k, tn), lambda i, j: (layer, 0, j)),
            pl.BlockSpec((tm, tn), lambda i, j: (i, j)),
            pl.BlockSpec((1, d), lambda i, j: (0, 0)),
            pl.BlockSpec((1, d), lambda i, j: (0, 0)),
        ],
        out_specs=[pl.BlockSpec(memory_space=pl.ANY), pl.BlockSpec(memory_space=pl.ANY)],
        out_shape=[jax.ShapeDtypeStruct((m, d), F32), jax.ShapeDtypeStruct((m, d), BF16)],
        scratch_shapes=[pltpu.VMEM((tm, d), F32), pltpu.VMEM((tm, d), F32),
                        pltpu.VMEM((tm, d), BF16), pltpu.SemaphoreType.DMA((2,))],
        compiler_params=pltpu.CompilerParams(
            dimension_semantics=("arbitrary", "arbitrary"),
            vmem_limit_bytes=_vmem_limit(nbytes)),
        name="proj_norm_res",
    )(*a_list, w, x, g, g_next)


def _xattn_kernel(h_ref, wq_ref, k_ref, v_ref, o_ref, wqb_ref, *, scale):
    @pl.when(pl.program_id(1) == 0)
    def _():
        wqb_ref[...] = wq_ref[...].astype(BF16)

    q = jnp.dot(h_ref[...], wqb_ref[...], preferred_element_type=F32).astype(BF16)
    s = lax.dot_general(q, k_ref[...], (((1,), (1,)), ((), ())),
                        preferred_element_type=F32) * scale
    p = jnp.exp(s - jnp.max(s, axis=-1, keepdims=True))
    p = (p / jnp.sum(p, axis=-1, keepdims=True)).astype(BF16)
    o_ref[...] = jnp.dot(p, v_ref[...], preferred_element_type=F32).astype(o_ref.dtype)


def _xattn(h, w_q, kv, layer, *, seq, tm):
    m, d = h.shape
    dh = d // XA_HEADS
    nbytes = (2 * tm * d * 2 + d * dh * 4 + d * dh * 2 + 8 * N_MEM * dh * 2 + 2 * tm * dh * 2
              + tm * dh * 6)
    return pl.pallas_call(
        functools.partial(_xattn_kernel, scale=float(dh) ** -0.5),
        grid=(XA_HEADS, m // tm),
        in_specs=[
            pl.BlockSpec((tm, d), lambda j, i: (i, 0)),
            pl.BlockSpec((None, d, dh), lambda j, i: (layer, 0, j), **_W_ONCE),
            pl.BlockSpec((N_MEM, dh), lambda j, i: ((i * tm) // seq, j)),
            pl.BlockSpec((N_MEM, dh), lambda j, i: ((i * tm) // seq, j + XA_HEADS)),
        ],
        out_specs=pl.BlockSpec((tm, dh), lambda j, i: (i, j)),
        out_shape=jax.ShapeDtypeStruct((m, d), BF16),
        scratch_shapes=[pltpu.VMEM((d, dh), BF16)],
        compiler_params=pltpu.CompilerParams(
            dimension_semantics=("arbitrary", "arbitrary"),
            vmem_limit_bytes=_vmem_limit(nbytes)),
        name="xattn",
    )(h, w_q, kv, kv)


def _mlp_kernel(*refs, emit_h):
    if emit_h:
        (h_ref, x_ref, wu_ref, wd_ref, g_ref, gn_ref, o_hbm, hn_hbm,
         o_scr, x_scr, hn_scr, sems) = refs
    else:
        (h_ref, x_ref, wu_ref, wd_ref, g_ref, o_hbm, o_scr, x_scr, sems) = refs
        gn_ref = hn_hbm = hn_scr = None
    i = pl.program_id(0)
    f = pl.program_id(1)
    tm, xw = x_ref.shape

    @pl.when(f == 0)
    def _():
        o_scr[...] = jnp.zeros(o_scr.shape, F32)

    x_scr[:, pl.ds(pl.multiple_of(f * xw, xw), xw)] = x_ref[...]
    u = jnp.dot(h_ref[...], wu_ref[...], preferred_element_type=F32)
    u = jnp.square(jnp.maximum(u, 0.0)).astype(BF16)
    o_scr[...] += jnp.dot(u, wd_ref[...], preferred_element_type=F32)

    @pl.when(f == pl.num_programs(1) - 1)
    def _():
        _post_norm_residual_to_hbm(o_scr, x_scr, g_ref, gn_ref, hn_scr, o_hbm, hn_hbm,
                                   i * tm, sems)


def _mlp(h, x, w_up, w_down, g_post, g_next, layer, *, tm, tf):
    m, d = x.shape
    d_ff = w_up.shape[2]
    nf = d_ff // tf
    xw = d // nf
    assert xw * nf == d and xw % LANES == 0
    emit_h = g_next is not None
    nbytes = (2 * tm * d * 2 + 2 * tm * xw * 4 + 4 * d * tf * 2 + 2 * tm * d * 4
              + (tm * d * 2 if emit_h else 0))
    in_specs = [
        pl.BlockSpec((tm, d), lambda i, f: (i, 0)),
        pl.BlockSpec((tm, xw), lambda i, f: (i, f)),
        pl.BlockSpec((None, d, tf), lambda i, f: (layer, 0, f)),
        pl.BlockSpec((None, tf, d), lambda i, f: (layer, f, 0)),
        pl.BlockSpec((1, d), lambda i, f: (0, 0)),
    ]
    out_specs = [pl.BlockSpec(memory_space=pl.ANY)]
    out_shape = [jax.ShapeDtypeStruct((m, d), F32)]
    scratch = [pltpu.VMEM((tm, d), F32), pltpu.VMEM((tm, d), F32)]
    args = [h, x, w_up, w_down, g_post]
    if emit_h:
        in_specs.append(pl.BlockSpec((1, d), lambda i, f: (0, 0)))
        out_specs.append(pl.BlockSpec(memory_space=pl.ANY))
        out_shape.append(jax.ShapeDtypeStruct((m, d), BF16))
        scratch.append(pltpu.VMEM((tm, d), BF16))
        args.append(g_next)
    scratch.append(pltpu.SemaphoreType.DMA((2,)))
    outs = pl.pallas_call(
        functools.partial(_mlp_kernel, emit_h=emit_h),
        grid=(m // tm, nf),
        in_specs=in_specs,
        out_specs=out_specs,
        out_shape=out_shape,
        scratch_shapes=scratch,
        compiler_params=pltpu.CompilerParams(
            dimension_semantics=("arbitrary", "arbitrary"),
            vmem_limit_bytes=_vmem_limit(nbytes)),
        name="mlp",
    )(*args)
    return (outs[0], outs[1]) if emit_h else (outs[0], None)


def kernel(x, mem, g_pre_mix, w_in, conv_w, conv_b, ln_a_g, ln_a_b, ln_v_g, ln_v_b, w_spatial, b_spatial, w_out, g_post_mix, g_pre_xa, g_mem, w_q, w_kv, w_o, g_post_xa, g_pre_mlp, w_up, w_down, g_post_mlp):
    batch, seq, d = x.shape
    depth = w_in.shape[0]
    w_a = conv_w.shape[2]
    w_b = ln_v_g.shape[1]
    m = batch * seq

    w_out_b = w_out.astype(BF16)
    w_o_b = w_o.astype(BF16)
    w_up_b = w_up.astype(BF16)
    w_down_b = w_down.astype(BF16)

    xf = x.reshape(m, d)
    memf = mem.reshape(batch * N_MEM, d)

    def row(p, l):
        return p[l].reshape(1, -1)

    h = _prenorm(xf, row(g_pre_mix, 0), tm=512)
    for l in range(depth):
        hglu = _inproj_glu(h, w_in, l, w_a=w_a, tm=1024, tn=512)
        y_a = _conv_ln_silu(hglu, conv_w[l], row(conv_b, l), row(ln_a_g, l), row(ln_a_b, l),
                            batch=batch, seq=seq)
        y_b = _inproj_gmlp(h, w_in, row(ln_v_g, l), row(ln_v_b, l), w_spatial,
                           b_spatial[l].reshape(H_B, CHUNK, 1), l, w_a=w_a, w_b=w_b, tm=1024)
        xf, h = _proj_norm_res([y_a, y_b], w_out_b, xf, row(g_post_mix, l),
                               row(g_pre_xa, l), l, tm=512, tn=512)
        kv = _norm_matmul(memf, row(g_mem, l), w_kv, l, tm=batch * N_MEM, tn=512,
                          out_dtype=BF16)
        o = _xattn(h, w_q, kv, l, seq=seq, tm=1024)
        xf, h = _proj_norm_res([o], w_o_b, xf, row(g_post_xa, l), row(g_pre_mlp, l), l,
                               tm=512, tn=512)
        g_next = row(g_pre_mix, l + 1) if l + 1 < depth else None
        xf, h = _mlp(h, xf, w_up_b, w_down_b, row(g_post_mlp, l), g_next, l, tm=512, tf=512)
    return xf.reshape(batch, seq, d)
```

```python
import functools

import jax
import jax.numpy as jnp
from jax import lax
from jax.experimental import pallas as pl
from jax.experimental.pallas import tpu as pltpu

F32 = jnp.float32
BF16 = jnp.bfloat16

EPS = 1e-6
H_A = 8
H_B = 8
CONV_K = 31
CHUNK = 128
N_MEM = 256
XA_HEADS = 4

V7X_VMEM_BYTES = 64 * 1024 * 1024
LANES = 128
SUBLANES = 8
CONV_PAD = 32
CONV_ROWS = 64
CONV_TAIL = CONV_ROWS + CONV_PAD
CONV_LN_ROWS = 256
NORM_ROWS = 64
NORM_COLS = 512


def _vmem_limit(nbytes):
    return int(min(nbytes + 12 * 1024 * 1024, V7X_VMEM_BYTES - 4 * 1024 * 1024))


def _lane_fold(v):
    acc = v[:, 0:LANES]
    for t in range(1, v.shape[1] // LANES):
        acc = acc + v[:, t * LANES:(t + 1) * LANES]
    return acc


def _row_rstd(src_ref, rows, d):
    part = None
    for s in range(d // NORM_COLS):
        a = src_ref[rows, s * NORM_COLS:(s + 1) * NORM_COLS]
        f = _lane_fold(a * a)
        part = f if part is None else part + f
    return lax.rsqrt(jnp.sum(part, axis=-1, keepdims=True) * (1.0 / d) + EPS)


def _scale_rows_to(src_ref, rows, rstd, g_ref, h_ref):
    d = src_ref.shape[1]
    for s in range(d // NORM_COLS):
        cols = slice(s * NORM_COLS, (s + 1) * NORM_COLS)
        h_ref[rows, cols] = (src_ref[rows, cols] * rstd * g_ref[:, cols]).astype(h_ref.dtype)


def _rmsnorm_rows_to(x_ref, g_ref, h_ref):
    tm, d = x_ref.shape

    def body(r, carry):
        rows = pl.ds(pl.multiple_of(r * NORM_ROWS, NORM_ROWS), NORM_ROWS)
        _scale_rows_to(x_ref, rows, _row_rstd(x_ref, rows, d), g_ref, h_ref)
        return carry

    lax.fori_loop(0, tm // NORM_ROWS, body, 0)


def _post_norm_residual_to_hbm(o_scr, x_scr, g_ref, gn_ref, h_scr, o_hbm, h_hbm, row0, sems):
    tm, d = o_scr.shape
    n = tm // NORM_ROWS

    def copies(r):
        rows = pl.ds(pl.multiple_of(r * NORM_ROWS, NORM_ROWS), NORM_ROWS)
        dst = pl.ds(pl.multiple_of(row0 + r * NORM_ROWS, NORM_ROWS), NORM_ROWS)
        out = [pltpu.make_async_copy(o_scr.at[rows, :], o_hbm.at[dst, :], sems.at[0])]
        if h_scr is not None:
            out.append(pltpu.make_async_copy(h_scr.at[rows, :], h_hbm.at[dst, :], sems.at[1]))
        return out

    def body(r, carry):
        rows = pl.ds(pl.multiple_of(r * NORM_ROWS, NORM_ROWS), NORM_ROWS)
        rstd = _row_rstd(o_scr, rows, d)
        part = None
        for s in range(d // NORM_COLS):
            cols = slice(s * NORM_COLS, (s + 1) * NORM_COLS)
            y = x_scr[rows, cols] + o_scr[rows, cols] * rstd * g_ref[:, cols]
            o_scr[rows, cols] = y
            if h_scr is not None:
                f = _lane_fold(y * y)
                part = f if part is None else part + f
        if h_scr is not None:
            rstd_n = lax.rsqrt(jnp.sum(part, axis=-1, keepdims=True) * (1.0 / d) + EPS)
            _scale_rows_to(o_scr, rows, rstd_n, gn_ref, h_scr)
        for c in copies(r):
            c.start()
        return carry

    lax.fori_loop(0, n, body, 0)

    def drain(r, carry):
        for c in copies(r):
            c.wait()
        return carry

    lax.fori_loop(0, n, drain, 0)


def _group_layernorm(v, g, b):
    mu = jnp.mean(v, axis=-1, keepdims=True)
    xc = v - mu
    var = jnp.mean(xc * xc, axis=-1, keepdims=True)
    return xc * lax.rsqrt(var + EPS) * g + b


def _prenorm_kernel(x_ref, g_ref, h_ref):
    _rmsnorm_rows_to(x_ref, g_ref, h_ref)


def _prenorm(x, g, *, tm):
    m, d = x.shape
    return pl.pallas_call(
        _prenorm_kernel,
        grid=(m // tm,),
        in_specs=[pl.BlockSpec((tm, d), lambda i: (i, 0)),
                  pl.BlockSpec((1, d), lambda i: (0, 0))],
        out_specs=pl.BlockSpec((tm, d), lambda i: (i, 0)),
        out_shape=jax.ShapeDtypeStruct((m, d), BF16),
        compiler_params=pltpu.CompilerParams(
            dimension_semantics=("arbitrary",),
            vmem_limit_bytes=_vmem_limit(2 * tm * d * 6)),
        name="prenorm",
    )(x, g)


def _norm_matmul_kernel(x_ref, g_ref, w_ref, o_ref, h_ref):
    @pl.when(pl.program_id(1) == 0)
    def _():
        _rmsnorm_rows_to(x_ref, g_ref, h_ref)

    o_ref[...] = jnp.dot(h_ref[...], w_ref[...].astype(BF16),
                         preferred_element_type=F32).astype(o_ref.dtype)


def _norm_matmul(x, g, w, layer, *, tm, tn, out_dtype):
    m, d = x.shape
    n = w.shape[2]
    nbytes = tm * d * 4 + tm * d * 2 + 2 * d * tn * 4 + d * tn * 2 + 2 * tm * tn * 4
    return pl.pallas_call(
        _norm_matmul_kernel,
        grid=(m // tm, n // tn),
        in_specs=[
            pl.BlockSpec((tm, d), lambda i, j: (i, 0), pipeline_mode=pl.Buffered(1)),
            pl.BlockSpec((1, d), lambda i, j: (0, 0)),
            pl.BlockSpec((None, d, tn), lambda i, j: (layer, 0, j)),
        ],
        out_specs=pl.BlockSpec((tm, tn), lambda i, j: (i, j)),
        out_shape=jax.ShapeDtypeStruct((m, n), out_dtype),
        scratch_shapes=[pltpu.VMEM((tm, d), BF16)],
        compiler_params=pltpu.CompilerParams(
            dimension_semantics=("arbitrary", "arbitrary"),
            vmem_limit_bytes=_vmem_limit(nbytes)),
        name="norm_matmul",
    )(x, g, w)


_W_ONCE = dict(pipeline_mode=pl.Buffered(1))


def _inproj_glu_kernel(h_ref, wa_ref, wg_ref, o_ref, wab_ref, wgb_ref):
    @pl.when(pl.program_id(1) == 0)
    def _():
        wab_ref[...] = wa_ref[...].astype(BF16)
        wgb_ref[...] = wg_ref[...].astype(BF16)

    h = h_ref[...]
    a = jnp.dot(h, wab_ref[...], preferred_element_type=F32)
    gate = jnp.dot(h, wgb_ref[...], preferred_element_type=F32)
    o_ref[...] = a * jax.nn.sigmoid(gate)


def _inproj_glu(h, w_in, layer, *, w_a, tm, tn):
    m, d = h.shape
    gate_off = w_a // tn
    nbytes = 2 * tm * d * 2 + 2 * d * tn * 4 + 2 * d * tn * 2 + 4 * tm * tn * 4
    return pl.pallas_call(
        _inproj_glu_kernel,
        grid=(w_a // tn, m // tm),
        in_specs=[
            pl.BlockSpec((tm, d), lambda j, i: (i, 0)),
            pl.BlockSpec((None, d, tn), lambda j, i: (layer, 0, j), **_W_ONCE),
            pl.BlockSpec((None, d, tn), lambda j, i: (layer, 0, j + gate_off), **_W_ONCE),
        ],
        out_specs=pl.BlockSpec((tm, tn), lambda j, i: (i, j)),
        out_shape=jax.ShapeDtypeStruct((m, w_a), F32),
        scratch_shapes=[pltpu.VMEM((d, tn), BF16), pltpu.VMEM((d, tn), BF16)],
        compiler_params=pltpu.CompilerParams(
            dimension_semantics=("arbitrary", "arbitrary"),
            vmem_limit_bytes=_vmem_limit(nbytes)),
        name="inproj_glu",
    )(h, w_in, w_in)


def _inproj_gmlp_kernel(h_ref, wu_ref, wv_ref, lg_ref, lb_ref, ws_ref, bs_ref, o_ref,
                        wub_ref, wvb_ref):
    @pl.when(pl.program_id(1) == 0)
    def _():
        wub_ref[...] = wu_ref[...].astype(BF16)
        wvb_ref[...] = wv_ref[...].astype(BF16)

    h = h_ref[...]
    u = jax.nn.gelu(jnp.dot(h, wub_ref[...], preferred_element_type=F32))
    v = jax.nn.gelu(jnp.dot(h, wvb_ref[...], preferred_element_type=F32))
    vn = _group_layernorm(v, lg_ref[...], lb_ref[...]).astype(BF16)

    t_idx = lax.broadcasted_iota(jnp.int32, (CHUNK, CHUNK), 0)
    s_idx = lax.broadcasted_iota(jnp.int32, (CHUNK, CHUNK), 1)
    w_sp = jnp.where(t_idx >= s_idx, ws_ref[...], 0.0).astype(BF16)
    bias = bs_ref[...]
    for c in range(o_ref.shape[0] // CHUNK):
        rows = slice(c * CHUNK, (c + 1) * CHUNK)
        s = jnp.dot(w_sp, vn[rows, :], preferred_element_type=F32)
        o_ref[rows, :] = (u[rows, :] * (s + bias)).astype(o_ref.dtype)


def _inproj_gmlp(h, w_in, ln_g, ln_b, w_sp, b_sp, layer, *, w_a, w_b, tm):
    m, d = h.shape
    d_b = w_b // H_B
    u_off = 2 * w_a // d_b
    v_off = u_off + H_B
    nbytes = (2 * tm * d * 2 + 4 * d * d_b * 4 + 2 * d * d_b * 2 + 2 * tm * d_b * 2
              + 4 * tm * d_b * 4)
    return pl.pallas_call(
        _inproj_gmlp_kernel,
        grid=(H_B, m // tm),
        in_specs=[
            pl.BlockSpec((tm, d), lambda j, i: (i, 0)),
            pl.BlockSpec((None, d, d_b), lambda j, i: (layer, 0, j + u_off)),
            pl.BlockSpec((None, d, d_b), lambda j, i: (layer, 0, j + v_off)),
            pl.BlockSpec((1, d_b), lambda j, i: (0, j)),
            pl.BlockSpec((1, d_b), lambda j, i: (0, j)),
            pl.BlockSpec((None, None, CHUNK, CHUNK), lambda j, i: (layer, j, 0, 0)),
            pl.BlockSpec((None, CHUNK, 1), lambda j, i: (j, 0, 0)),
        ],
        out_specs=pl.BlockSpec((tm, d_b), lambda j, i: (i, j)),
        out_shape=jax.ShapeDtypeStruct((m, w_b), BF16),
        scratch_shapes=[pltpu.VMEM((d, d_b), BF16), pltpu.VMEM((d, d_b), BF16)],
        compiler_params=pltpu.CompilerParams(
            dimension_semantics=("arbitrary", "arbitrary"),
            vmem_limit_bytes=_vmem_limit(nbytes)),
        name="inproj_gmlp",
    )(h, w_in, w_in, ln_g, ln_b, w_sp, b_sp)


def _conv_ln_silu_kernel(x_ref, w_ref, cb_ref, lg_ref, lb_ref, o_ref, p_ref, a_ref):
    seq, c = x_ref.shape
    rows = CONV_ROWS
    p_ref[0, 0:CONV_PAD, :] = jnp.zeros((CONV_PAD, c), F32)
    p_ref[0, CONV_PAD + seq:, :] = jnp.zeros((CONV_TAIL, c), F32)
    p_ref[0, CONV_PAD:CONV_PAD + seq, :] = x_ref[...]

    def build(i, carry):
        t0 = pl.multiple_of(i * rows, rows)
        win = p_ref[0, pl.ds(t0, rows + SUBLANES), :]
        for r in range(1, SUBLANES):
            p_ref[r, pl.ds(t0, rows), :] = win[r:r + rows, :]
        return carry

    lax.fori_loop(0, (seq + rows) // rows, build, 0)
    shift = CONV_PAD - (CONV_K - 1)

    def taps(i, carry):
        t0 = pl.multiple_of(i * rows, rows)
        acc = jnp.broadcast_to(cb_ref[...], (rows, c))
        for k in range(CONV_K):
            q, r = divmod(shift + k, SUBLANES)
            start = pl.multiple_of(t0 + SUBLANES * q, SUBLANES)
            acc = acc + w_ref[k:k + 1, :] * p_ref[r, pl.ds(start, rows), :]
        a_ref[pl.ds(t0, rows), :] = acc
        return carry

    lax.fori_loop(0, seq // rows, taps, 0)

    def finish(i, carry):
        t0 = pl.multiple_of(i * CONV_LN_ROWS, CONV_LN_ROWS)
        y = _group_layernorm(a_ref[pl.ds(t0, CONV_LN_ROWS), :], lg_ref[...], lb_ref[...])
        o_ref[pl.ds(t0, CONV_LN_ROWS), :] = (y * jax.nn.sigmoid(y)).astype(o_ref.dtype)
        return carry

    lax.fori_loop(0, seq // CONV_LN_ROWS, finish, 0)


def _conv_ln_silu(hglu, conv_w, conv_b, ln_g, ln_b, *, batch, seq):
    m, w_a = hglu.shape
    cg = w_a // H_A
    p_rows = CONV_PAD + seq + CONV_TAIL
    return pl.pallas_call(
        _conv_ln_silu_kernel,
        grid=(batch, H_A),
        in_specs=[
            pl.BlockSpec((seq, cg), lambda b, c: (b, c)),
            pl.BlockSpec((CONV_K, cg), lambda b, c: (0, c)),
            pl.BlockSpec((1, cg), lambda b, c: (0, c)),
            pl.BlockSpec((1, cg), lambda b, c: (0, c)),
            pl.BlockSpec((1, cg), lambda b, c: (0, c)),
        ],
        out_specs=pl.BlockSpec((seq, cg), lambda b, c: (b, c)),
        out_shape=jax.ShapeDtypeStruct((m, w_a), BF16),
        scratch_shapes=[pltpu.VMEM((SUBLANES, p_rows, cg), F32), pltpu.VMEM((seq, cg), F32)],
        compiler_params=pltpu.CompilerParams(
            dimension_semantics=("arbitrary", "arbitrary"),
            vmem_limit_bytes=_vmem_limit(
                SUBLANES * p_rows * cg * 4 + seq * cg * 4 + 2 * seq * cg * 6)),
        name="conv_ln_silu",
    )(hglu, conv_w, conv_b, ln_g, ln_b)


def _proj_norm_res_kernel(*refs, n_a):
    a_refs = refs[:n_a]
    (w_ref, x_ref, g_ref, gn_ref, o_hbm, h_hbm, o_scr, x_scr, h_scr, sems) = refs[n_a:]
    i = pl.program_id(0)
    j = pl.program_id(1)
    tm, tn = x_ref.shape
    acc = None
    off = 0
    for a_ref in a_refs:
        ka = a_ref.shape[1]
        part = jnp.dot(a_ref[...], w_ref[off:off + ka, :], preferred_element_type=F32)
        acc = part if acc is None else acc + part
        off += ka
    cols = pl.ds(pl.multiple_of(j * tn, tn), tn)
    o_scr[:, cols] = acc
    x_scr[:, cols] = x_ref[...]

    @pl.when(j == pl.num_programs(1) - 1)
    def _():
        _post_norm_residual_to_hbm(o_scr, x_scr, g_ref, gn_ref, h_scr, o_hbm, h_hbm,
                                   i * tm, sems)


def _proj_norm_res(a_list, w, x, g, g_next, *, tm, tn):
    m, d = x.shape
    k = w.shape[0]
    n_a = len(a_list)
    nbytes = 2 * tm * k * 2 + 2 * k * tn * 2 + 2 * tm * tn * 4 + 2 * tm * d * 4 + tm * d * 2
    a_specs = [pl.BlockSpec((tm, a.shape[1]), lambda i, j: (i, 0)) for a in a_list]
    return pl.pallas_call(
        functools.partial(_proj_norm_res_kernel, n_a=n_a),
        grid=(m // tm, d // tn),
        in_specs=a_specs + [
            pl.BlockSpec((k, tn), lambda i, j: (0, j)),
            pl.BlockSpec((tm, tn), lambda i, j: (i, j)),
            pl.BlockSpec((1, d), lambda i, j: (0, 0)),
            pl.BlockSpec((1, d), lambda i, j: (0, 0)),
        ],
        out_specs=[pl.BlockSpec(memory_space=pl.ANY), pl.BlockSpec(memory_space=pl.ANY)],
        out_shape=[jax.ShapeDtypeStruct((m, d), F32), jax.ShapeDtypeStruct((m, d), BF16)],
        scratch_shapes=[pltpu.VMEM((tm, d), F32), pltpu.VMEM((tm, d), F32),
                        pltpu.VMEM((tm, d), BF16), pltpu.SemaphoreType.DMA((2,))],
        compiler_params=pltpu.CompilerParams(
            dimension_semantics=("arbitrary", "arbitrary"),
            vmem_limit_bytes=_vmem_limit(nbytes)),
        name="proj_norm_res",
    )(*a_list, w, x, g, g_next)


def _xattn_kernel(h_ref, wq_ref, k_ref, v_ref, o_ref, wqb_ref, *, scale):
    @pl.when(pl.program_id(1) == 0)
    def _():
        wqb_ref[...] = wq_ref[...].astype(BF16)

    q = jnp.dot(h_ref[...], wqb_ref[...], preferred_element_type=F32).astype(BF16)
    s = lax.dot_general(q, k_ref[...], (((1,), (1,)), ((), ())),
                        preferred_element_type=F32) * scale
    p = jnp.exp(s - jnp.max(s, axis=-1, keepdims=True))
    p = (p / jnp.sum(p, axis=-1, keepdims=True)).astype(BF16)
    o_ref[...] = jnp.dot(p, v_ref[...], preferred_element_type=F32).astype(o_ref.dtype)


def _xattn(h, w_q, kv, layer, *, seq, tm):
    m, d = h.shape
    dh = d // XA_HEADS
    nbytes = (2 * tm * d * 2 + d * dh * 4 + d * dh * 2 + 8 * N_MEM * dh * 2 + 2 * tm * dh * 2
              + tm * dh * 6)
    return pl.pallas_call(
        functools.partial(_xattn_kernel, scale=float(dh) ** -0.5),
        grid=(XA_HEADS, m // tm),
        in_specs=[
            pl.BlockSpec((tm, d), lambda j, i: (i, 0)),
            pl.BlockSpec((None, d, dh), lambda j, i: (layer, 0, j), **_W_ONCE),
            pl.BlockSpec((N_MEM, dh), lambda j, i: ((i * tm) // seq, j)),
            pl.BlockSpec((N_MEM, dh), lambda j, i: ((i * tm) // seq, j + XA_HEADS)),
        ],
        out_specs=pl.BlockSpec((tm, dh), lambda j, i: (i, j)),
        out_shape=jax.ShapeDtypeStruct((m, d), BF16),
        scratch_shapes=[pltpu.VMEM((d, dh), BF16)],
        compiler_params=pltpu.CompilerParams(
            dimension_semantics=("arbitrary", "arbitrary"),
            vmem_limit_bytes=_vmem_limit(nbytes)),
        name="xattn",
    )(h, w_q, kv, kv)


def _mlp_kernel(*refs, n_cast):
    if n_cast:
        h_ref, x_ref, wu_ref, wd_ref, g_ref, gn_ref = refs[:6]
        cast_in = refs[6:6 + n_cast]
        o_hbm, hn_hbm = refs[6 + n_cast:8 + n_cast]
        cast_out = refs[8 + n_cast:8 + 2 * n_cast]
        o_scr, x_scr, hn_scr, sems = refs[8 + 2 * n_cast:]
    else:
        (h_ref, x_ref, wu_ref, wd_ref, g_ref, o_hbm, o_scr, x_scr, sems) = refs
        gn_ref = hn_hbm = hn_scr = None
        cast_in = cast_out = ()
    i = pl.program_id(0)
    f = pl.program_id(1)
    tm, xw = x_ref.shape

    @pl.when(f == 0)
    def _():
        o_scr[...] = jnp.zeros(o_scr.shape, F32)

    for src, dst in zip(cast_in, cast_out):
        dst[...] = src[...].astype(dst.dtype)
    x_scr[:, pl.ds(pl.multiple_of(f * xw, xw), xw)] = x_ref[...]
    u = jnp.dot(h_ref[...], wu_ref[...], preferred_element_type=F32)
    u = jnp.square(jnp.maximum(u, 0.0)).astype(BF16)
    o_scr[...] += jnp.dot(u, wd_ref[...], preferred_element_type=F32)

    @pl.when(f == pl.num_programs(1) - 1)
    def _():
        _post_norm_residual_to_hbm(o_scr, x_scr, g_ref, gn_ref, hn_scr, o_hbm, hn_hbm,
                                   i * tm, sems)


def _mlp(h, x, w_up, w_down, g_post, nxt, *, tm, tf):
    m, d = x.shape
    d_ff = w_up.shape[1]
    ni, nf = m // tm, d_ff // tf
    xw = d // nf
    assert xw * nf == d and xw % LANES == 0
    nbytes = 2 * tm * d * 2 + 2 * tm * xw * 4 + 4 * d * tf * 2 + 2 * tm * d * 4
    in_specs = [
        pl.BlockSpec((tm, d), lambda i, f: (i, 0)),
        pl.BlockSpec((tm, xw), lambda i, f: (i, f)),
        pl.BlockSpec((d, tf), lambda i, f: (0, f)),
        pl.BlockSpec((tf, d), lambda i, f: (f, 0)),
        pl.BlockSpec((1, d), lambda i, f: (0, 0)),
    ]
    out_specs = [pl.BlockSpec(memory_space=pl.ANY)]
    out_shape = [jax.ShapeDtypeStruct((m, d), F32)]
    scratch = [pltpu.VMEM((tm, d), F32), pltpu.VMEM((tm, d), F32)]
    args = [h, x, w_up, w_down, g_post]
    n_cast = 0
    if nxt is not None:
        g_next, next_layer, cast_ws = nxt
        n_cast = len(cast_ws)
        in_specs.append(pl.BlockSpec((1, d), lambda i, f: (0, 0)))
        args.append(g_next)
        out_specs.append(pl.BlockSpec(memory_space=pl.ANY))
        out_shape.append(jax.ShapeDtypeStruct((m, d), BF16))
        scratch.append(pltpu.VMEM((tm, d), BF16))
        nbytes += tm * d * 2
        for w in cast_ws:
            _, r, c = w.shape
            br, bc = r // ni, c // nf
            assert br * ni == r and bc * nf == c and br % 16 == 0 and bc % LANES == 0
            in_specs.append(pl.BlockSpec((None, br, bc), lambda i, f: (next_layer, i, f)))
            args.append(w)
            out_specs.append(pl.BlockSpec((br, bc), lambda i, f: (i, f)))
            out_shape.append(jax.ShapeDtypeStruct((r, c), BF16))
            nbytes += 2 * br * bc * 6
    scratch.append(pltpu.SemaphoreType.DMA((2,)))
    outs = pl.pallas_call(
        functools.partial(_mlp_kernel, n_cast=n_cast),
        grid=(ni, nf),
        in_specs=in_specs,
        out_specs=out_specs,
        out_shape=out_shape,
        scratch_shapes=scratch,
        compiler_params=pltpu.CompilerParams(
            dimension_semantics=("arbitrary", "arbitrary"),
            vmem_limit_bytes=_vmem_limit(nbytes)),
        name="mlp",
    )(*args)
    if nxt is None:
        return outs[0], None, []
    return outs[0], outs[1], list(outs[2:])


def kernel(x, mem, g_pre_mix, w_in, conv_w, conv_b, ln_a_g, ln_a_b, ln_v_g, ln_v_b, w_spatial, b_spatial, w_out, g_post_mix, g_pre_xa, g_mem, w_q, w_kv, w_o, g_post_xa, g_pre_mlp, w_up, w_down, g_post_mlp):
    batch, seq, d = x.shape
    depth = w_in.shape[0]
    w_a = conv_w.shape[2]
    w_b = ln_v_g.shape[1]
    m = batch * seq

    restreamed = [w_out, w_o, w_up, w_down]
    w_out_b, w_o_b, w_up_b, w_down_b = [w[0].astype(BF16) for w in restreamed]

    xf = x.reshape(m, d)
    memf = mem.reshape(batch * N_MEM, d)

    def row(p, l):
        return p[l].reshape(1, -1)

    h = _prenorm(xf, row(g_pre_mix, 0), tm=512)
    for l in range(depth):
        hglu = _inproj_glu(h, w_in, l, w_a=w_a, tm=1024, tn=512)
        y_a = _conv_ln_silu(hglu, conv_w[l], row(conv_b, l), row(ln_a_g, l), row(ln_a_b, l),
                            batch=batch, seq=seq)
        y_b = _inproj_gmlp(h, w_in, row(ln_v_g, l), row(ln_v_b, l), w_spatial,
                           b_spatial[l].reshape(H_B, CHUNK, 1), l, w_a=w_a, w_b=w_b, tm=1024)
        xf, h = _proj_norm_res([y_a, y_b], w_out_b, xf, row(g_post_mix, l),
                               row(g_pre_xa, l), tm=512, tn=1024)
        kv = _norm_matmul(memf, row(g_mem, l), w_kv, l, tm=batch * N_MEM, tn=512,
                          out_dtype=BF16)
        o = _xattn(h, w_q, kv, l, seq=seq, tm=1024)
        xf, h = _proj_norm_res([o], w_o_b, xf, row(g_post_xa, l), row(g_pre_mlp, l),
                               tm=512, tn=1024)
        nxt = (row(g_pre_mix, l + 1), l + 1, restreamed) if l + 1 < depth else None
        xf, h, cast = _mlp(h, xf, w_up_b, w_down_b, row(g_post_mlp, l), nxt, tm=512, tf=512)
        if cast:
            w_out_b, w_o_b, w_up_b, w_down_b = cast
    return xf.reshape(batch, seq, d)
```

```python
import functools

import jax
import jax.numpy as jnp
from jax import lax
from jax.experimental import pallas as pl
from jax.experimental.pallas import tpu as pltpu

F32 = jnp.float32
BF16 = jnp.bfloat16

EPS = 1e-6
H_A = 8
H_B = 8
CONV_K = 31
CHUNK = 128
N_MEM = 256
XA_HEADS = 4

V7X_VMEM_BYTES = 64 * 1024 * 1024
LANES = 128
SUBLANES = 8
CONV_PAD = 32
CONV_ROWS = 64
CONV_TAIL = CONV_ROWS + CONV_PAD
CONV_LN_ROWS = 256
NORM_ROWS = 64
NORM_COLS = 512


def _vmem_limit(nbytes):
    return int(min(nbytes + 12 * 1024 * 1024, V7X_VMEM_BYTES - 4 * 1024 * 1024))


def _lane_fold(v):
    acc = v[:, 0:LANES]
    for t in range(1, v.shape[1] // LANES):
        acc = acc + v[:, t * LANES:(t + 1) * LANES]
    return acc


def _row_rstd(src_ref, rows, d):
    part = None
    for s in range(d // NORM_COLS):
        a = src_ref[rows, s * NORM_COLS:(s + 1) * NORM_COLS]
        f = _lane_fold(a * a)
        part = f if part is None else part + f
    return lax.rsqrt(jnp.sum(part, axis=-1, keepdims=True) * (1.0 / d) + EPS)


def _scale_rows_to(src_ref, rows, rstd, g_ref, h_ref):
    d = src_ref.shape[1]
    for s in range(d // NORM_COLS):
        cols = slice(s * NORM_COLS, (s + 1) * NORM_COLS)
        h_ref[rows, cols] = (src_ref[rows, cols] * rstd * g_ref[:, cols]).astype(h_ref.dtype)


def _rmsnorm_rows_to(x_ref, g_ref, h_ref):
    tm, d = x_ref.shape

    def body(r, carry):
        rows = pl.ds(pl.multiple_of(r * NORM_ROWS, NORM_ROWS), NORM_ROWS)
        _scale_rows_to(x_ref, rows, _row_rstd(x_ref, rows, d), g_ref, h_ref)
        return carry

    lax.fori_loop(0, tm // NORM_ROWS, body, 0)


class _EpilogueCopies:
    def __init__(self, o_scr, x_hbm, o_hbm, h_hbm, xbuf, hbuf, sems, row0):
        self.o_scr, self.x_hbm, self.o_hbm, self.h_hbm = o_scr, x_hbm, o_hbm, h_hbm
        self.xbuf, self.hbuf, self.sems, self.row0 = xbuf, hbuf, sems, row0

    def _hbm_rows(self, r):
        return pl.ds(pl.multiple_of(self.row0 + r * NORM_ROWS, NORM_ROWS), NORM_ROWS)

    def x_in(self, r, slot):
        return pltpu.make_async_copy(self.x_hbm.at[self._hbm_rows(r), :], self.xbuf.at[slot],
                                     self.sems.at[slot])

    def o_out(self, r):
        rows = pl.ds(pl.multiple_of(r * NORM_ROWS, NORM_ROWS), NORM_ROWS)
        return pltpu.make_async_copy(self.o_scr.at[rows, :], self.o_hbm.at[self._hbm_rows(r), :],
                                     self.sems.at[2])

    def h_out(self, r, slot):
        return pltpu.make_async_copy(self.hbuf.at[slot], self.h_hbm.at[self._hbm_rows(r), :],
                                     self.sems.at[3 + slot])


def _post_norm_residual_to_hbm(cp, g_ref, gn_ref):
    o_scr, xbuf, hbuf = cp.o_scr, cp.xbuf, cp.hbuf
    tm, d = o_scr.shape
    n = tm // NORM_ROWS
    emit_h = cp.h_hbm is not None

    def body(r, carry):
        slot = lax.rem(r, 2)
        rows = pl.ds(pl.multiple_of(r * NORM_ROWS, NORM_ROWS), NORM_ROWS)
        cp.x_in(r, slot).wait()

        @pl.when(r + 1 < n)
        def _():
            cp.x_in(r + 1, 1 - slot).start()

        rstd = _row_rstd(o_scr, rows, d)
        part = None
        for s in range(d // NORM_COLS):
            cols = slice(s * NORM_COLS, (s + 1) * NORM_COLS)
            y = xbuf[slot, :, cols] + o_scr[rows, cols] * rstd * g_ref[:, cols]
            o_scr[rows, cols] = y
            if emit_h:
                f = _lane_fold(y * y)
                part = f if part is None else part + f
        cp.o_out(r).start()
        if emit_h:
            rstd_n = lax.rsqrt(jnp.sum(part, axis=-1, keepdims=True) * (1.0 / d) + EPS)

            @pl.when(r >= 2)
            def _():
                cp.h_out(r - 2, slot).wait()

            for s in range(d // NORM_COLS):
                cols = slice(s * NORM_COLS, (s + 1) * NORM_COLS)
                hbuf[slot, :, cols] = (o_scr[rows, cols] * rstd_n * gn_ref[:, cols]
                                       ).astype(hbuf.dtype)
            cp.h_out(r, slot).start()
        return carry

    lax.fori_loop(0, n, body, 0)

    def drain(r, carry):
        cp.o_out(r).wait()
        return carry

    lax.fori_loop(0, n, drain, 0)
    if emit_h:
        for r in (n - 2, n - 1):
            cp.h_out(r, r % 2).wait()


def _group_layernorm(v, g, b):
    mu = jnp.mean(v, axis=-1, keepdims=True)
    xc = v - mu
    var = jnp.mean(xc * xc, axis=-1, keepdims=True)
    return xc * lax.rsqrt(var + EPS) * g + b


def _prenorm_kernel(x_ref, g_ref, h_ref):
    _rmsnorm_rows_to(x_ref, g_ref, h_ref)


def _prenorm(x, g, *, tm):
    m, d = x.shape
    return pl.pallas_call(
        _prenorm_kernel,
        grid=(m // tm,),
        in_specs=[pl.BlockSpec((tm, d), lambda i: (i, 0)),
                  pl.BlockSpec((1, d), lambda i: (0, 0))],
        out_specs=pl.BlockSpec((tm, d), lambda i: (i, 0)),
        out_shape=jax.ShapeDtypeStruct((m, d), BF16),
        compiler_params=pltpu.CompilerParams(
            dimension_semantics=("arbitrary",),
            vmem_limit_bytes=_vmem_limit(2 * tm * d * 6)),
        name="prenorm",
    )(x, g)


def _norm_matmul_kernel(x_ref, g_ref, w_ref, o_ref, h_ref):
    @pl.when(pl.program_id(1) == 0)
    def _():
        _rmsnorm_rows_to(x_ref, g_ref, h_ref)

    o_ref[...] = jnp.dot(h_ref[...], w_ref[...].astype(BF16),
                         preferred_element_type=F32).astype(o_ref.dtype)


def _norm_matmul(x, g, w, layer, *, tm, tn, out_dtype):
    m, d = x.shape
    n = w.shape[2]
    nbytes = tm * d * 4 + tm * d * 2 + 2 * d * tn * 4 + d * tn * 2 + 2 * tm * tn * 4
    return pl.pallas_call(
        _norm_matmul_kernel,
        grid=(m // tm, n // tn),
        in_specs=[
            pl.BlockSpec((tm, d), lambda i, j: (i, 0), pipeline_mode=pl.Buffered(1)),
            pl.BlockSpec((1, d), lambda i, j: (0, 0)),
            pl.BlockSpec((None, d, tn), lambda i, j: (layer, 0, j)),
        ],
        out_specs=pl.BlockSpec((tm, tn), lambda i, j: (i, j)),
        out_shape=jax.ShapeDtypeStruct((m, n), out_dtype),
        scratch_shapes=[pltpu.VMEM((tm, d), BF16)],
        compiler_params=pltpu.CompilerParams(
            dimension_semantics=("arbitrary", "arbitrary"),
            vmem_limit_bytes=_vmem_limit(nbytes)),
        name="norm_matmul",
    )(x, g, w)


_W_ONCE = dict(pipeline_mode=pl.Buffered(1))


def _inproj_glu_kernel(h_ref, wa_ref, wg_ref, o_ref, wab_ref, wgb_ref):
    @pl.when(pl.program_id(1) == 0)
    def _():
        wab_ref[...] = wa_ref[...].astype(BF16)
        wgb_ref[...] = wg_ref[...].astype(BF16)

    h = h_ref[...]
    a = jnp.dot(h, wab_ref[...], preferred_element_type=F32)
    gate = jnp.dot(h, wgb_ref[...], preferred_element_type=F32)
    o_ref[...] = a * jax.nn.sigmoid(gate)


def _inproj_glu(h, w_in, layer, *, w_a, tm, tn):
    m, d = h.shape
    gate_off = w_a // tn
    nbytes = 2 * tm * d * 2 + 2 * d * tn * 4 + 2 * d * tn * 2 + 4 * tm * tn * 4
    return pl.pallas_call(
        _inproj_glu_kernel,
        grid=(w_a // tn, m // tm),
        in_specs=[
            pl.BlockSpec((tm, d), lambda j, i: (i, 0)),
            pl.BlockSpec((None, d, tn), lambda j, i: (layer, 0, j), **_W_ONCE),
            pl.BlockSpec((None, d, tn), lambda j, i: (layer, 0, j + gate_off), **_W_ONCE),
        ],
        out_specs=pl.BlockSpec((tm, tn), lambda j, i: (i, j)),
        out_shape=jax.ShapeDtypeStruct((m, w_a), F32),
        scratch_shapes=[pltpu.VMEM((d, tn), BF16), pltpu.VMEM((d, tn), BF16)],
        compiler_params=pltpu.CompilerParams(
            dimension_semantics=("arbitrary", "arbitrary"),
            vmem_limit_bytes=_vmem_limit(nbytes)),
        name="inproj_glu",
    )(h, w_in, w_in)


def _inproj_gmlp_kernel(h_ref, wu_ref, wv_ref, lg_ref, lb_ref, ws_ref, bs_ref, o_ref,
                        wub_ref, wvb_ref):
    @pl.when(pl.program_id(1) == 0)
    def _():
        wub_ref[...] = wu_ref[...].astype(BF16)
        wvb_ref[...] = wv_ref[...].astype(BF16)

    h = h_ref[...]
    u = jax.nn.gelu(jnp.dot(h, wub_ref[...], preferred_element_type=F32))
    v = jax.nn.gelu(jnp.dot(h, wvb_ref[...], preferred_element_type=F32))
    vn = _group_layernorm(v, lg_ref[...], lb_ref[...]).astype(BF16)

    t_idx = lax.broadcasted_iota(jnp.int32, (CHUNK, CHUNK), 0)
    s_idx = lax.broadcasted_iota(jnp.int32, (CHUNK, CHUNK), 1)
    w_sp = jnp.where(t_idx >= s_idx, ws_ref[...], 0.0).astype(BF16)
    bias = bs_ref[...]
    for c in range(o_ref.shape[0] // CHUNK):
        rows = slice(c * CHUNK, (c + 1) * CHUNK)
        s = jnp.dot(w_sp, vn[rows, :], preferred_element_type=F32)
        o_ref[rows, :] = (u[rows, :] * (s + bias)).astype(o_ref.dtype)


def _inproj_gmlp(h, w_in, ln_g, ln_b, w_sp, b_sp, layer, *, w_a, w_b, tm):
    m, d = h.shape
    d_b = w_b // H_B
    u_off = 2 * w_a // d_b
    v_off = u_off + H_B
    nbytes = (2 * tm * d * 2 + 4 * d * d_b * 4 + 2 * d * d_b * 2 + 2 * tm * d_b * 2
              + 4 * tm * d_b * 4)
    return pl.pallas_call(
        _inproj_gmlp_kernel,
        grid=(H_B, m // tm),
        in_specs=[
            pl.BlockSpec((tm, d), lambda j, i: (i, 0)),
            pl.BlockSpec((None, d, d_b), lambda j, i: (layer, 0, j + u_off)),
            pl.BlockSpec((None, d, d_b), lambda j, i: (layer, 0, j + v_off)),
            pl.BlockSpec((1, d_b), lambda j, i: (0, j)),
            pl.BlockSpec((1, d_b), lambda j, i: (0, j)),
            pl.BlockSpec((None, None, CHUNK, CHUNK), lambda j, i: (layer, j, 0, 0)),
            pl.BlockSpec((None, CHUNK, 1), lambda j, i: (j, 0, 0)),
        ],
        out_specs=pl.BlockSpec((tm, d_b), lambda j, i: (i, j)),
        out_shape=jax.ShapeDtypeStruct((m, w_b), BF16),
        scratch_shapes=[pltpu.VMEM((d, d_b), BF16), pltpu.VMEM((d, d_b), BF16)],
        compiler_params=pltpu.CompilerParams(
            dimension_semantics=("arbitrary", "arbitrary"),
            vmem_limit_bytes=_vmem_limit(nbytes)),
        name="inproj_gmlp",
    )(h, w_in, w_in, ln_g, ln_b, w_sp, b_sp)


def _conv_ln_silu_kernel(x_ref, w_ref, cb_ref, lg_ref, lb_ref, o_ref, p_ref, a_ref):
    seq, c = x_ref.shape
    rows = CONV_ROWS
    p_ref[0, 0:CONV_PAD, :] = jnp.zeros((CONV_PAD, c), F32)
    p_ref[0, CONV_PAD + seq:, :] = jnp.zeros((CONV_TAIL, c), F32)
    p_ref[0, CONV_PAD:CONV_PAD + seq, :] = x_ref[...]

    def build(i, carry):
        t0 = pl.multiple_of(i * rows, rows)
        win = p_ref[0, pl.ds(t0, rows + SUBLANES), :]
        for r in range(1, SUBLANES):
            p_ref[r, pl.ds(t0, rows), :] = win[r:r + rows, :]
        return carry

    lax.fori_loop(0, (seq + rows) // rows, build, 0)
    shift = CONV_PAD - (CONV_K - 1)

    def taps(i, carry):
        t0 = pl.multiple_of(i * rows, rows)
        acc = jnp.broadcast_to(cb_ref[...], (rows, c))
        for k in range(CONV_K):
            q, r = divmod(shift + k, SUBLANES)
            start = pl.multiple_of(t0 + SUBLANES * q, SUBLANES)
            acc = acc + w_ref[k:k + 1, :] * p_ref[r, pl.ds(start, rows), :]
        a_ref[pl.ds(t0, rows), :] = acc
        return carry

    lax.fori_loop(0, seq // rows, taps, 0)

    def finish(i, carry):
        t0 = pl.multiple_of(i * CONV_LN_ROWS, CONV_LN_ROWS)
        y = _group_layernorm(a_ref[pl.ds(t0, CONV_LN_ROWS), :], lg_ref[...], lb_ref[...])
        o_ref[pl.ds(t0, CONV_LN_ROWS), :] = (y * jax.nn.sigmoid(y)).astype(o_ref.dtype)
        return carry

    lax.fori_loop(0, seq // CONV_LN_ROWS, finish, 0)


def _conv_ln_silu(hglu, conv_w, conv_b, ln_g, ln_b, *, batch, seq):
    m, w_a = hglu.shape
    cg = w_a // H_A
    p_rows = CONV_PAD + seq + CONV_TAIL
    return pl.pallas_call(
        _conv_ln_silu_kernel,
        grid=(batch, H_A),
        in_specs=[
            pl.BlockSpec((seq, cg), lambda b, c: (b, c)),
            pl.BlockSpec((CONV_K, cg), lambda b, c: (0, c)),
            pl.BlockSpec((1, cg), lambda b, c: (0, c)),
            pl.BlockSpec((1, cg), lambda b, c: (0, c)),
            pl.BlockSpec((1, cg), lambda b, c: (0, c)),
        ],
        out_specs=pl.BlockSpec((seq, cg), lambda b, c: (b, c)),
        out_shape=jax.ShapeDtypeStruct((m, w_a), BF16),
        scratch_shapes=[pltpu.VMEM((SUBLANES, p_rows, cg), F32), pltpu.VMEM((seq, cg), F32)],
        compiler_params=pltpu.CompilerParams(
            dimension_semantics=("arbitrary", "arbitrary"),
            vmem_limit_bytes=_vmem_limit(
                SUBLANES * p_rows * cg * 4 + seq * cg * 4 + 2 * seq * cg * 6)),
        name="conv_ln_silu",
    )(hglu, conv_w, conv_b, ln_g, ln_b)


def _proj_norm_res_kernel(*refs, n_a):
    a_refs = refs[:n_a]
    (w_ref, x_hbm, g_ref, gn_ref, o_hbm, h_hbm, o_scr, xbuf, hbuf, sems) = refs[n_a:]
    i = pl.program_id(0)
    j = pl.program_id(1)
    last = pl.num_programs(1) - 1
    tm = o_scr.shape[0]
    tn = w_ref.shape[1]
    cp = _EpilogueCopies(o_scr, x_hbm, o_hbm, h_hbm, xbuf, hbuf, sems, i * tm)

    @pl.when(j == last)
    def _():
        cp.x_in(0, 0).start()

    acc = None
    off = 0
    for a_ref in a_refs:
        ka = a_ref.shape[1]
        part = jnp.dot(a_ref[...], w_ref[off:off + ka, :], preferred_element_type=F32)
        acc = part if acc is None else acc + part
        off += ka
    o_scr[:, pl.ds(pl.multiple_of(j * tn, tn), tn)] = acc

    @pl.when(j == last)
    def _():
        _post_norm_residual_to_hbm(cp, g_ref, gn_ref)


def _epilogue_scratch(tm, d, emit_h=True):
    scratch = [pltpu.VMEM((tm, d), F32), pltpu.VMEM((2, NORM_ROWS, d), F32)]
    if emit_h:
        scratch.append(pltpu.VMEM((2, NORM_ROWS, d), BF16))
    return scratch + [pltpu.SemaphoreType.DMA((5,))]


def _proj_norm_res(a_list, w, x, g, g_next, *, tm, tn):
    m, d = x.shape
    k = w.shape[0]
    n_a = len(a_list)
    nbytes = 2 * tm * k * 2 + 2 * k * tn * 2 + tm * d * 4 + 2 * NORM_ROWS * d * 6
    a_specs = [pl.BlockSpec((tm, a.shape[1]), lambda i, j: (i, 0)) for a in a_list]
    return pl.pallas_call(
        functools.partial(_proj_norm_res_kernel, n_a=n_a),
        grid=(m // tm, d // tn),
        in_specs=a_specs + [
            pl.BlockSpec((k, tn), lambda i, j: (0, j)),
            pl.BlockSpec(memory_space=pl.ANY),
            pl.BlockSpec((1, d), lambda i, j: (0, 0)),
            pl.BlockSpec((1, d), lambda i, j: (0, 0)),
        ],
        out_specs=[pl.BlockSpec(memory_space=pl.ANY), pl.BlockSpec(memory_space=pl.ANY)],
        out_shape=[jax.ShapeDtypeStruct((m, d), F32), jax.ShapeDtypeStruct((m, d), BF16)],
        scratch_shapes=_epilogue_scratch(tm, d),
        compiler_params=pltpu.CompilerParams(
            dimension_semantics=("arbitrary", "arbitrary"),
            vmem_limit_bytes=_vmem_limit(nbytes)),
        name="proj_norm_res",
    )(*a_list, w, x, g, g_next)


def _xattn_kernel(h_ref, wq_ref, k_ref, v_ref, o_ref, wqb_ref, *, scale):
    @pl.when(pl.program_id(1) == 0)
    def _():
        wqb_ref[...] = wq_ref[...].astype(BF16)

    q = jnp.dot(h_ref[...], wqb_ref[...], preferred_element_type=F32).astype(BF16)
    s = lax.dot_general(q, k_ref[...], (((1,), (1,)), ((), ())),
                        preferred_element_type=F32) * scale
    p = jnp.exp(s - jnp.max(s, axis=-1, keepdims=True))
    p = (p / jnp.sum(p, axis=-1, keepdims=True)).astype(BF16)
    o_ref[...] = jnp.dot(p, v_ref[...], preferred_element_type=F32).astype(o_ref.dtype)


def _xattn(h, w_q, kv, layer, *, seq, tm):
    m, d = h.shape
    dh = d // XA_HEADS
    nbytes = (2 * tm * d * 2 + d * dh * 4 + d * dh * 2 + 8 * N_MEM * dh * 2 + 2 * tm * dh * 2
              + tm * dh * 6)
    return pl.pallas_call(
        functools.partial(_xattn_kernel, scale=float(dh) ** -0.5),
        grid=(XA_HEADS, m // tm),
        in_specs=[
            pl.BlockSpec((tm, d), lambda j, i: (i, 0)),
            pl.BlockSpec((None, d, dh), lambda j, i: (layer, 0, j), **_W_ONCE),
            pl.BlockSpec((N_MEM, dh), lambda j, i: ((i * tm) // seq, j)),
            pl.BlockSpec((N_MEM, dh), lambda j, i: ((i * tm) // seq, j + XA_HEADS)),
        ],
        out_specs=pl.BlockSpec((tm, dh), lambda j, i: (i, j)),
        out_shape=jax.ShapeDtypeStruct((m, d), BF16),
        scratch_shapes=[pltpu.VMEM((d, dh), BF16)],
        compiler_params=pltpu.CompilerParams(
            dimension_semantics=("arbitrary", "arbitrary"),
            vmem_limit_bytes=_vmem_limit(nbytes)),
        name="xattn",
    )(h, w_q, kv, kv)


def _mlp_kernel(*refs, n_cast):
    if n_cast:
        h_ref, x_hbm, wu_ref, wd_ref, g_ref, gn_ref = refs[:6]
        cast_in = refs[6:6 + n_cast]
        o_hbm, hn_hbm = refs[6 + n_cast:8 + n_cast]
        cast_out = refs[8 + n_cast:8 + 2 * n_cast]
        o_scr, xbuf, hbuf, sems = refs[8 + 2 * n_cast:]
    else:
        (h_ref, x_hbm, wu_ref, wd_ref, g_ref, o_hbm, o_scr, xbuf, sems) = refs
        gn_ref = hn_hbm = hbuf = None
        cast_in = cast_out = ()
    i = pl.program_id(0)
    f = pl.program_id(1)
    last = pl.num_programs(1) - 1
    cp = _EpilogueCopies(o_scr, x_hbm, o_hbm, hn_hbm, xbuf, hbuf, sems, i * o_scr.shape[0])

    @pl.when(f == 0)
    def _():
        o_scr[...] = jnp.zeros(o_scr.shape, F32)

    @pl.when(f == last)
    def _():
        cp.x_in(0, 0).start()

    for src, dst in zip(cast_in, cast_out):
        dst[...] = src[...].astype(dst.dtype)
    u = jnp.dot(h_ref[...], wu_ref[...], preferred_element_type=F32)
    u = jnp.square(jnp.maximum(u, 0.0)).astype(BF16)
    o_scr[...] += jnp.dot(u, wd_ref[...], preferred_element_type=F32)

    @pl.when(f == last)
    def _():
        _post_norm_residual_to_hbm(cp, g_ref, gn_ref)


def _mlp(h, x, w_up, w_down, g_post, nxt, *, tm, tf):
    m, d = x.shape
    d_ff = w_up.shape[1]
    ni, nf = m // tm, d_ff // tf
    nbytes = tm * d * 2 + 4 * d * tf * 2 + tm * d * 4 + 2 * NORM_ROWS * d * 6 + tm * tf * 6
    in_specs = [
        pl.BlockSpec((tm, d), lambda i, f: (i, 0), pipeline_mode=pl.Buffered(1)),
        pl.BlockSpec(memory_space=pl.ANY),
        pl.BlockSpec((d, tf), lambda i, f: (0, f)),
        pl.BlockSpec((tf, d), lambda i, f: (f, 0)),
        pl.BlockSpec((1, d), lambda i, f: (0, 0)),
    ]
    out_specs = [pl.BlockSpec(memory_space=pl.ANY)]
    out_shape = [jax.ShapeDtypeStruct((m, d), F32)]
    args = [h, x, w_up, w_down, g_post]
    n_cast = 0
    if nxt is not None:
        g_next, next_layer, cast_ws = nxt
        n_cast = len(cast_ws)
        in_specs.append(pl.BlockSpec((1, d), lambda i, f: (0, 0)))
        args.append(g_next)
        out_specs.append(pl.BlockSpec(memory_space=pl.ANY))
        out_shape.append(jax.ShapeDtypeStruct((m, d), BF16))
        for w in cast_ws:
            _, r, c = w.shape
            br, bc = r // ni, c // nf
            assert br * ni == r and bc * nf == c and br % 16 == 0 and bc % LANES == 0
            in_specs.append(pl.BlockSpec((None, br, bc), lambda i, f: (next_layer, i, f)))
            args.append(w)
            out_specs.append(pl.BlockSpec((br, bc), lambda i, f: (i, f)))
            out_shape.append(jax.ShapeDtypeStruct((r, c), BF16))
            nbytes += 2 * br * bc * 6
    outs = pl.pallas_call(
        functools.partial(_mlp_kernel, n_cast=n_cast),
        grid=(ni, nf),
        in_specs=in_specs,
        out_specs=out_specs,
        out_shape=out_shape,
        scratch_shapes=_epilogue_scratch(tm, d, emit_h=nxt is not None),
        compiler_params=pltpu.CompilerParams(
            dimension_semantics=("arbitrary", "arbitrary"),
            vmem_limit_bytes=_vmem_limit(nbytes)),
        name="mlp",
    )(*args)
    if nxt is None:
        return outs[0], None, []
    return outs[0], outs[1], list(outs[2:])


def kernel(x, mem, g_pre_mix, w_in, conv_w, conv_b, ln_a_g, ln_a_b, ln_v_g, ln_v_b, w_spatial, b_spatial, w_out, g_post_mix, g_pre_xa, g_mem, w_q, w_kv, w_o, g_post_xa, g_pre_mlp, w_up, w_down, g_post_mlp):
    batch, seq, d = x.shape
    depth = w_in.shape[0]
    w_a = conv_w.shape[2]
    w_b = ln_v_g.shape[1]
    m = batch * seq

    restreamed = [w_out, w_o, w_up, w_down]
    w_out_b, w_o_b, w_up_b, w_down_b = [w[0].astype(BF16) for w in restreamed]

    xf = x.reshape(m, d)
    memf = mem.reshape(batch * N_MEM, d)

    def row(p, l):
        return p[l].reshape(1, -1)

    h = _prenorm(xf, row(g_pre_mix, 0), tm=512)
    for l in range(depth):
        hglu = _inproj_glu(h, w_in, l, w_a=w_a, tm=1024, tn=512)
        y_a = _conv_ln_silu(hglu, conv_w[l], row(conv_b, l), row(ln_a_g, l), row(ln_a_b, l),
                            batch=batch, seq=seq)
        y_b = _inproj_gmlp(h, w_in, row(ln_v_g, l), row(ln_v_b, l), w_spatial,
                           b_spatial[l].reshape(H_B, CHUNK, 1), l, w_a=w_a, w_b=w_b, tm=1024)
        xf, h = _proj_norm_res([y_a, y_b], w_out_b, xf, row(g_post_mix, l),
                               row(g_pre_xa, l), tm=1024, tn=512)
        kv = _norm_matmul(memf, row(g_mem, l), w_kv, l, tm=batch * N_MEM, tn=512,
                          out_dtype=BF16)
        o = _xattn(h, w_q, kv, l, seq=seq, tm=1024)
        xf, h = _proj_norm_res([o], w_o_b, xf, row(g_post_xa, l), row(g_pre_mlp, l),
                               tm=1024, tn=512)
        nxt = (row(g_pre_mix, l + 1), l + 1, restreamed) if l + 1 < depth else None
        xf, h, cast = _mlp(h, xf, w_up_b, w_down_b, row(g_post_mlp, l), nxt, tm=1024, tf=512)
        if cast:
            w_out_b, w_o_b, w_up_b, w_down_b = cast
    return xf.reshape(batch, seq, d)
```

```python
import functools

import jax
import jax.numpy as jnp
from jax import lax
from jax.experimental import pallas as pl
from jax.experimental.pallas import tpu as pltpu

F32 = jnp.float32
BF16 = jnp.bfloat16

EPS = 1e-6
H_A = 8
H_B = 8
CONV_K = 31
CHUNK = 128
N_MEM = 256
XA_HEADS = 4

V7X_VMEM_BYTES = 64 * 1024 * 1024
LANES = 128
SUBLANES = 8
CONV_PAD = 32
CONV_ROWS = 64
CONV_TAIL = CONV_ROWS + CONV_PAD
CONV_LN_ROWS = 256
NORM_ROWS = 64
NORM_COLS = 512


def _vmem_limit(nbytes):
    return int(min(nbytes + 12 * 1024 * 1024, V7X_VMEM_BYTES - 4 * 1024 * 1024))


def _lane_fold(v):
    acc = v[:, 0:LANES]
    for t in range(1, v.shape[1] // LANES):
        acc = acc + v[:, t * LANES:(t + 1) * LANES]
    return acc


def _row_rstd(src_ref, rows, d):
    part = None
    for s in range(d // NORM_COLS):
        a = src_ref[rows, s * NORM_COLS:(s + 1) * NORM_COLS]
        f = _lane_fold(a * a)
        part = f if part is None else part + f
    return lax.rsqrt(jnp.sum(part, axis=-1, keepdims=True) * (1.0 / d) + EPS)


def _scale_rows_to(src_ref, rows, rstd, g_ref, h_ref):
    d = src_ref.shape[1]
    for s in range(d // NORM_COLS):
        cols = slice(s * NORM_COLS, (s + 1) * NORM_COLS)
        h_ref[rows, cols] = (src_ref[rows, cols] * rstd * g_ref[:, cols]).astype(h_ref.dtype)


def _rmsnorm_rows_to(x_ref, g_ref, h_ref):
    tm, d = x_ref.shape

    def body(r, carry):
        rows = pl.ds(pl.multiple_of(r * NORM_ROWS, NORM_ROWS), NORM_ROWS)
        _scale_rows_to(x_ref, rows, _row_rstd(x_ref, rows, d), g_ref, h_ref)
        return carry

    lax.fori_loop(0, tm // NORM_ROWS, body, 0)


def _post_norm_residual_to_hbm(o_scr, x_scr, g_ref, gn_ref, h_scr, o_hbm, h_hbm, row0, sems):
    tm, d = o_scr.shape
    n = tm // NORM_ROWS

    def copies(r):
        rows = pl.ds(pl.multiple_of(r * NORM_ROWS, NORM_ROWS), NORM_ROWS)
        dst = pl.ds(pl.multiple_of(row0 + r * NORM_ROWS, NORM_ROWS), NORM_ROWS)
        return [pltpu.make_async_copy(o_scr.at[rows, :], o_hbm.at[dst, :], sems.at[0]),
                pltpu.make_async_copy(h_scr.at[rows, :], h_hbm.at[dst, :], sems.at[1])]

    def body(r, carry):
        rows = pl.ds(pl.multiple_of(r * NORM_ROWS, NORM_ROWS), NORM_ROWS)
        rstd = _row_rstd(o_scr, rows, d)
        part = None
        for s in range(d // NORM_COLS):
            cols = slice(s * NORM_COLS, (s + 1) * NORM_COLS)
            y = x_scr[rows, cols] + o_scr[rows, cols] * rstd * g_ref[:, cols]
            o_scr[rows, cols] = y
            f = _lane_fold(y * y)
            part = f if part is None else part + f
        rstd_n = lax.rsqrt(jnp.sum(part, axis=-1, keepdims=True) * (1.0 / d) + EPS)
        _scale_rows_to(o_scr, rows, rstd_n, gn_ref, h_scr)
        for c in copies(r):
            c.start()
        return carry

    lax.fori_loop(0, n, body, 0)

    def drain(r, carry):
        for c in copies(r):
            c.wait()
        return carry

    lax.fori_loop(0, n, drain, 0)


X_SLOTS = 4


class _EpilogueCopies:
    def __init__(self, o_scr, x_hbm, o_hbm, h_hbm, xbuf, hbuf, sems, row0):
        self.o_scr, self.x_hbm, self.o_hbm, self.h_hbm = o_scr, x_hbm, o_hbm, h_hbm
        self.xbuf, self.hbuf, self.sems, self.row0 = xbuf, hbuf, sems, row0

    def _hbm_rows(self, r):
        return pl.ds(pl.multiple_of(self.row0 + r * NORM_ROWS, NORM_ROWS), NORM_ROWS)

    def x_in(self, r, slot):
        return pltpu.make_async_copy(self.x_hbm.at[self._hbm_rows(r), :], self.xbuf.at[slot],
                                     self.sems.at[slot])

    def o_out(self, r):
        rows = pl.ds(pl.multiple_of(r * NORM_ROWS, NORM_ROWS), NORM_ROWS)
        return pltpu.make_async_copy(self.o_scr.at[rows, :], self.o_hbm.at[self._hbm_rows(r), :],
                                     self.sems.at[X_SLOTS])

    def h_out(self, r, slot):
        return pltpu.make_async_copy(self.hbuf.at[slot], self.h_hbm.at[self._hbm_rows(r), :],
                                     self.sems.at[X_SLOTS + 1 + slot])


def _post_norm_residual_streamed(cp, g_ref, gn_ref):
    o_scr, xbuf, hbuf = cp.o_scr, cp.xbuf, cp.hbuf
    tm, d = o_scr.shape
    n = tm // NORM_ROWS
    emit_h = cp.h_hbm is not None

    def body(r, carry):
        xslot = lax.rem(r, X_SLOTS)
        hslot = lax.rem(r, 2)
        rows = pl.ds(pl.multiple_of(r * NORM_ROWS, NORM_ROWS), NORM_ROWS)
        cp.x_in(r, xslot).wait()
        rstd = _row_rstd(o_scr, rows, d)
        part = None
        for s in range(d // NORM_COLS):
            cols = slice(s * NORM_COLS, (s + 1) * NORM_COLS)
            y = xbuf[xslot, :, cols] + o_scr[rows, cols] * rstd * g_ref[:, cols]
            o_scr[rows, cols] = y
            if emit_h:
                f = _lane_fold(y * y)
                part = f if part is None else part + f
        cp.o_out(r).start()

        @pl.when(r + X_SLOTS < n)
        def _():
            cp.x_in(r + X_SLOTS, xslot).start()

        if emit_h:
            rstd_n = lax.rsqrt(jnp.sum(part, axis=-1, keepdims=True) * (1.0 / d) + EPS)

            @pl.when(r >= 2)
            def _():
                cp.h_out(r - 2, hslot).wait()

            for s in range(d // NORM_COLS):
                cols = slice(s * NORM_COLS, (s + 1) * NORM_COLS)
                hbuf[hslot, :, cols] = (o_scr[rows, cols] * rstd_n * gn_ref[:, cols]
                                        ).astype(hbuf.dtype)
            cp.h_out(r, hslot).start()
        return carry

    lax.fori_loop(0, n, body, 0)

    def drain(r, carry):
        cp.o_out(r).wait()
        return carry

    lax.fori_loop(0, n, drain, 0)
    if emit_h:
        for r in (n - 2, n - 1):
            cp.h_out(r, r % 2).wait()


def _group_layernorm(v, g, b):
    mu = jnp.mean(v, axis=-1, keepdims=True)
    xc = v - mu
    var = jnp.mean(xc * xc, axis=-1, keepdims=True)
    return xc * lax.rsqrt(var + EPS) * g + b


def _prenorm_kernel(x_ref, g_ref, h_ref):
    _rmsnorm_rows_to(x_ref, g_ref, h_ref)


def _prenorm(x, g, *, tm):
    m, d = x.shape
    return pl.pallas_call(
        _prenorm_kernel,
        grid=(m // tm,),
        in_specs=[pl.BlockSpec((tm, d), lambda i: (i, 0)),
                  pl.BlockSpec((1, d), lambda i: (0, 0))],
        out_specs=pl.BlockSpec((tm, d), lambda i: (i, 0)),
        out_shape=jax.ShapeDtypeStruct((m, d), BF16),
        compiler_params=pltpu.CompilerParams(
            dimension_semantics=("arbitrary",),
            vmem_limit_bytes=_vmem_limit(2 * tm * d * 6)),
        name="prenorm",
    )(x, g)


def _norm_matmul_kernel(x_ref, g_ref, w_ref, o_ref, h_ref):
    @pl.when(pl.program_id(1) == 0)
    def _():
        _rmsnorm_rows_to(x_ref, g_ref, h_ref)

    o_ref[...] = jnp.dot(h_ref[...], w_ref[...].astype(BF16),
                         preferred_element_type=F32).astype(o_ref.dtype)


def _norm_matmul(x, g, w, layer, *, tm, tn, out_dtype):
    m, d = x.shape
    n = w.shape[2]
    nbytes = tm * d * 4 + tm * d * 2 + 2 * d * tn * 4 + d * tn * 2 + 2 * tm * tn * 4
    return pl.pallas_call(
        _norm_matmul_kernel,
        grid=(m // tm, n // tn),
        in_specs=[
            pl.BlockSpec((tm, d), lambda i, j: (i, 0), pipeline_mode=pl.Buffered(1)),
            pl.BlockSpec((1, d), lambda i, j: (0, 0)),
            pl.BlockSpec((None, d, tn), lambda i, j: (layer, 0, j)),
        ],
        out_specs=pl.BlockSpec((tm, tn), lambda i, j: (i, j)),
        out_shape=jax.ShapeDtypeStruct((m, n), out_dtype),
        scratch_shapes=[pltpu.VMEM((tm, d), BF16)],
        compiler_params=pltpu.CompilerParams(
            dimension_semantics=("arbitrary", "arbitrary"),
            vmem_limit_bytes=_vmem_limit(nbytes)),
        name="norm_matmul",
    )(x, g, w)


_W_ONCE = dict(pipeline_mode=pl.Buffered(1))


def _inproj_glu_kernel(h_ref, wa_ref, wg_ref, o_ref, wab_ref, wgb_ref):
    @pl.when(pl.program_id(1) == 0)
    def _():
        wab_ref[...] = wa_ref[...].astype(BF16)
        wgb_ref[...] = wg_ref[...].astype(BF16)

    h = h_ref[...]
    a = jnp.dot(h, wab_ref[...], preferred_element_type=F32)
    gate = jnp.dot(h, wgb_ref[...], preferred_element_type=F32)
    o_ref[...] = a * jax.nn.sigmoid(gate)


def _inproj_glu(h, w_in, layer, *, w_a, tm, tn):
    m, d = h.shape
    gate_off = w_a // tn
    nbytes = 2 * tm * d * 2 + 2 * d * tn * 4 + 2 * d * tn * 2 + 4 * tm * tn * 4
    return pl.pallas_call(
        _inproj_glu_kernel,
        grid=(w_a // tn, m // tm),
        in_specs=[
            pl.BlockSpec((tm, d), lambda j, i: (i, 0)),
            pl.BlockSpec((None, d, tn), lambda j, i: (layer, 0, j), **_W_ONCE),
            pl.BlockSpec((None, d, tn), lambda j, i: (layer, 0, j + gate_off), **_W_ONCE),
        ],
        out_specs=pl.BlockSpec((tm, tn), lambda j, i: (i, j)),
        out_shape=jax.ShapeDtypeStruct((m, w_a), F32),
        scratch_shapes=[pltpu.VMEM((d, tn), BF16), pltpu.VMEM((d, tn), BF16)],
        compiler_params=pltpu.CompilerParams(
            dimension_semantics=("arbitrary", "arbitrary"),
            vmem_limit_bytes=_vmem_limit(nbytes)),
        name="inproj_glu",
    )(h, w_in, w_in)


def _inproj_gmlp_kernel(h_ref, wu_ref, wv_ref, lg_ref, lb_ref, ws_ref, bs_ref, o_ref,
                        wub_ref, wvb_ref):
    @pl.when(pl.program_id(1) == 0)
    def _():
        wub_ref[...] = wu_ref[...].astype(BF16)
        wvb_ref[...] = wv_ref[...].astype(BF16)

    h = h_ref[...]
    u = jax.nn.gelu(jnp.dot(h, wub_ref[...], preferred_element_type=F32))
    v = jax.nn.gelu(jnp.dot(h, wvb_ref[...], preferred_element_type=F32))
    vn = _group_layernorm(v, lg_ref[...], lb_ref[...]).astype(BF16)

    t_idx = lax.broadcasted_iota(jnp.int32, (CHUNK, CHUNK), 0)
    s_idx = lax.broadcasted_iota(jnp.int32, (CHUNK, CHUNK), 1)
    w_sp = jnp.where(t_idx >= s_idx, ws_ref[...], 0.0).astype(BF16)
    bias = bs_ref[...]
    for c in range(o_ref.shape[0] // CHUNK):
        rows = slice(c * CHUNK, (c + 1) * CHUNK)
        s = jnp.dot(w_sp, vn[rows, :], preferred_element_type=F32)
        o_ref[rows, :] = (u[rows, :] * (s + bias)).astype(o_ref.dtype)


def _inproj_gmlp(h, w_in, ln_g, ln_b, w_sp, b_sp, layer, *, w_a, w_b, tm):
    m, d = h.shape
    d_b = w_b // H_B
    u_off = 2 * w_a // d_b
    v_off = u_off + H_B
    nbytes = (2 * tm * d * 2 + 4 * d * d_b * 4 + 2 * d * d_b * 2 + 2 * tm * d_b * 2
              + 4 * tm * d_b * 4)
    return pl.pallas_call(
        _inproj_gmlp_kernel,
        grid=(H_B, m // tm),
        in_specs=[
            pl.BlockSpec((tm, d), lambda j, i: (i, 0)),
            pl.BlockSpec((None, d, d_b), lambda j, i: (layer, 0, j + u_off)),
            pl.BlockSpec((None, d, d_b), lambda j, i: (layer, 0, j + v_off)),
            pl.BlockSpec((1, d_b), lambda j, i: (0, j)),
            pl.BlockSpec((1, d_b), lambda j, i: (0, j)),
            pl.BlockSpec((None, None, CHUNK, CHUNK), lambda j, i: (layer, j, 0, 0)),
            pl.BlockSpec((None, CHUNK, 1), lambda j, i: (j, 0, 0)),
        ],
        out_specs=pl.BlockSpec((tm, d_b), lambda j, i: (i, j)),
        out_shape=jax.ShapeDtypeStruct((m, w_b), BF16),
        scratch_shapes=[pltpu.VMEM((d, d_b), BF16), pltpu.VMEM((d, d_b), BF16)],
        compiler_params=pltpu.CompilerParams(
            dimension_semantics=("arbitrary", "arbitrary"),
            vmem_limit_bytes=_vmem_limit(nbytes)),
        name="inproj_gmlp",
    )(h, w_in, w_in, ln_g, ln_b, w_sp, b_sp)


def _conv_ln_silu_kernel(x_ref, w_ref, cb_ref, lg_ref, lb_ref, o_ref, p_ref, a_ref):
    seq, c = x_ref.shape
    rows = CONV_ROWS
    p_ref[0, 0:CONV_PAD, :] = jnp.zeros((CONV_PAD, c), F32)
    p_ref[0, CONV_PAD + seq:, :] = jnp.zeros((CONV_TAIL, c), F32)
    p_ref[0, CONV_PAD:CONV_PAD + seq, :] = x_ref[...]

    def build(i, carry):
        t0 = pl.multiple_of(i * rows, rows)
        win = p_ref[0, pl.ds(t0, rows + SUBLANES), :]
        for r in range(1, SUBLANES):
            p_ref[r, pl.ds(t0, rows), :] = win[r:r + rows, :]
        return carry

    lax.fori_loop(0, (seq + rows) // rows, build, 0)
    shift = CONV_PAD - (CONV_K - 1)

    def taps(i, carry):
        t0 = pl.multiple_of(i * rows, rows)
        acc = jnp.broadcast_to(cb_ref[...], (rows, c))
        for k in range(CONV_K):
            q, r = divmod(shift + k, SUBLANES)
            start = pl.multiple_of(t0 + SUBLANES * q, SUBLANES)
            acc = acc + w_ref[k:k + 1, :] * p_ref[r, pl.ds(start, rows), :]
        a_ref[pl.ds(t0, rows), :] = acc
        return carry

    lax.fori_loop(0, seq // rows, taps, 0)

    def finish(i, carry):
        t0 = pl.multiple_of(i * CONV_LN_ROWS, CONV_LN_ROWS)
        y = _group_layernorm(a_ref[pl.ds(t0, CONV_LN_ROWS), :], lg_ref[...], lb_ref[...])
        o_ref[pl.ds(t0, CONV_LN_ROWS), :] = (y * jax.nn.sigmoid(y)).astype(o_ref.dtype)
        return carry

    lax.fori_loop(0, seq // CONV_LN_ROWS, finish, 0)


def _conv_ln_silu(hglu, conv_w, conv_b, ln_g, ln_b, *, batch, seq):
    m, w_a = hglu.shape
    cg = w_a // H_A
    p_rows = CONV_PAD + seq + CONV_TAIL
    return pl.pallas_call(
        _conv_ln_silu_kernel,
        grid=(batch, H_A),
        in_specs=[
            pl.BlockSpec((seq, cg), lambda b, c: (b, c)),
            pl.BlockSpec((CONV_K, cg), lambda b, c: (0, c)),
            pl.BlockSpec((1, cg), lambda b, c: (0, c)),
            pl.BlockSpec((1, cg), lambda b, c: (0, c)),
            pl.BlockSpec((1, cg), lambda b, c: (0, c)),
        ],
        out_specs=pl.BlockSpec((seq, cg), lambda b, c: (b, c)),
        out_shape=jax.ShapeDtypeStruct((m, w_a), BF16),
        scratch_shapes=[pltpu.VMEM((SUBLANES, p_rows, cg), F32), pltpu.VMEM((seq, cg), F32)],
        compiler_params=pltpu.CompilerParams(
            dimension_semantics=("arbitrary", "arbitrary"),
            vmem_limit_bytes=_vmem_limit(
                SUBLANES * p_rows * cg * 4 + seq * cg * 4 + 2 * seq * cg * 6)),
        name="conv_ln_silu",
    )(hglu, conv_w, conv_b, ln_g, ln_b)


def _proj_norm_res_kernel(*refs, n_a):
    a_refs = refs[:n_a]
    (w_ref, x_ref, g_ref, gn_ref, o_hbm, h_hbm, o_scr, x_scr, h_scr, sems) = refs[n_a:]
    i = pl.program_id(0)
    j = pl.program_id(1)
    tm, tn = x_ref.shape
    acc = None
    off = 0
    for a_ref in a_refs:
        ka = a_ref.shape[1]
        part = jnp.dot(a_ref[...], w_ref[off:off + ka, :], preferred_element_type=F32)
        acc = part if acc is None else acc + part
        off += ka
    cols = pl.ds(pl.multiple_of(j * tn, tn), tn)
    o_scr[:, cols] = acc
    x_scr[:, cols] = x_ref[...]

    @pl.when(j == pl.num_programs(1) - 1)
    def _():
        _post_norm_residual_to_hbm(o_scr, x_scr, g_ref, gn_ref, h_scr, o_hbm, h_hbm,
                                   i * tm, sems)


def _proj_norm_res(a_list, w, x, g, g_next, *, tm, tn):
    m, d = x.shape
    k = w.shape[0]
    n_a = len(a_list)
    nbytes = 2 * tm * k * 2 + 2 * k * tn * 2 + 2 * tm * tn * 4 + 2 * tm * d * 4 + tm * d * 2
    a_specs = [pl.BlockSpec((tm, a.shape[1]), lambda i, j: (i, 0)) for a in a_list]
    return pl.pallas_call(
        functools.partial(_proj_norm_res_kernel, n_a=n_a),
        grid=(m // tm, d // tn),
        in_specs=a_specs + [
            pl.BlockSpec((k, tn), lambda i, j: (0, j)),
            pl.BlockSpec((tm, tn), lambda i, j: (i, j)),
            pl.BlockSpec((1, d), lambda i, j: (0, 0)),
            pl.BlockSpec((1, d), lambda i, j: (0, 0)),
        ],
        out_specs=[pl.BlockSpec(memory_space=pl.ANY), pl.BlockSpec(memory_space=pl.ANY)],
        out_shape=[jax.ShapeDtypeStruct((m, d), F32), jax.ShapeDtypeStruct((m, d), BF16)],
        scratch_shapes=[pltpu.VMEM((tm, d), F32), pltpu.VMEM((tm, d), F32),
                        pltpu.VMEM((tm, d), BF16), pltpu.SemaphoreType.DMA((2,))],
        compiler_params=pltpu.CompilerParams(
            dimension_semantics=("arbitrary", "arbitrary"),
            vmem_limit_bytes=_vmem_limit(nbytes)),
        name="proj_norm_res",
    )(*a_list, w, x, g, g_next)


def _xattn_kernel(h_ref, wq_ref, k_ref, v_ref, o_ref, wqb_ref, *, scale):
    @pl.when(pl.program_id(1) == 0)
    def _():
        wqb_ref[...] = wq_ref[...].astype(BF16)

    q = jnp.dot(h_ref[...], wqb_ref[...], preferred_element_type=F32).astype(BF16)
    s = lax.dot_general(q, k_ref[...], (((1,), (1,)), ((), ())),
                        preferred_element_type=F32) * scale
    p = jnp.exp(s - jnp.max(s, axis=-1, keepdims=True))
    p = (p / jnp.sum(p, axis=-1, keepdims=True)).astype(BF16)
    o_ref[...] = jnp.dot(p, v_ref[...], preferred_element_type=F32).astype(o_ref.dtype)


def _xattn(h, w_q, kv, layer, *, seq, tm):
    m, d = h.shape
    dh = d // XA_HEADS
    nbytes = (2 * tm * d * 2 + d * dh * 4 + d * dh * 2 + 8 * N_MEM * dh * 2 + 2 * tm * dh * 2
              + tm * dh * 6)
    return pl.pallas_call(
        functools.partial(_xattn_kernel, scale=float(dh) ** -0.5),
        grid=(XA_HEADS, m // tm),
        in_specs=[
            pl.BlockSpec((tm, d), lambda j, i: (i, 0)),
            pl.BlockSpec((None, d, dh), lambda j, i: (layer, 0, j), **_W_ONCE),
            pl.BlockSpec((N_MEM, dh), lambda j, i: ((i * tm) // seq, j)),
            pl.BlockSpec((N_MEM, dh), lambda j, i: ((i * tm) // seq, j + XA_HEADS)),
        ],
        out_specs=pl.BlockSpec((tm, dh), lambda j, i: (i, j)),
        out_shape=jax.ShapeDtypeStruct((m, d), BF16),
        scratch_shapes=[pltpu.VMEM((d, dh), BF16)],
        compiler_params=pltpu.CompilerParams(
            dimension_semantics=("arbitrary", "arbitrary"),
            vmem_limit_bytes=_vmem_limit(nbytes)),
        name="xattn",
    )(h, w_q, kv, kv)


def _mlp_kernel(*refs, n_cast):
    if n_cast:
        h_ref, x_hbm, wu_ref, wd_ref, g_ref, gn_ref = refs[:6]
        cast_in = refs[6:6 + n_cast]
        o_hbm, hn_hbm = refs[6 + n_cast:8 + n_cast]
        cast_out = refs[8 + n_cast:8 + 2 * n_cast]
        o_scr, xbuf, hbuf, sems = refs[8 + 2 * n_cast:]
    else:
        (h_ref, x_hbm, wu_ref, wd_ref, g_ref, o_hbm, o_scr, xbuf, sems) = refs
        gn_ref = hn_hbm = hbuf = None
        cast_in = cast_out = ()
    i = pl.program_id(0)
    f = pl.program_id(1)
    last = pl.num_programs(1) - 1
    cp = _EpilogueCopies(o_scr, x_hbm, o_hbm, hn_hbm, xbuf, hbuf, sems, i * o_scr.shape[0])

    @pl.when(f == 0)
    def _():
        o_scr[...] = jnp.zeros(o_scr.shape, F32)

    @pl.when(f == last)
    def _():
        for r in range(X_SLOTS):
            cp.x_in(r, r).start()

    for src, dst in zip(cast_in, cast_out):
        dst[...] = src[...].astype(dst.dtype)
    u = jnp.dot(h_ref[...], wu_ref[...], preferred_element_type=F32)
    u = jnp.square(jnp.maximum(u, 0.0)).astype(BF16)
    o_scr[...] += jnp.dot(u, wd_ref[...], preferred_element_type=F32)

    @pl.when(f == last)
    def _():
        _post_norm_residual_streamed(cp, g_ref, gn_ref)


def _streamed_epilogue_scratch(tm, d, emit_h):
    scratch = [pltpu.VMEM((tm, d), F32), pltpu.VMEM((X_SLOTS, NORM_ROWS, d), F32)]
    if emit_h:
        scratch.append(pltpu.VMEM((2, NORM_ROWS, d), BF16))
    return scratch + [pltpu.SemaphoreType.DMA((X_SLOTS + 3,))]


def _mlp(h, x, w_up, w_down, g_post, nxt, *, tm, tf):
    m, d = x.shape
    d_ff = w_up.shape[1]
    ni, nf = m // tm, d_ff // tf
    nbytes = (tm * d * 2 + 4 * d * tf * 2 + tm * d * 4 + (X_SLOTS + 1) * NORM_ROWS * d * 4
              + tm * tf * 6)
    in_specs = [
        pl.BlockSpec((tm, d), lambda i, f: (i, 0), pipeline_mode=pl.Buffered(1)),
        pl.BlockSpec(memory_space=pl.ANY),
        pl.BlockSpec((d, tf), lambda i, f: (0, f)),
        pl.BlockSpec((tf, d), lambda i, f: (f, 0)),
        pl.BlockSpec((1, d), lambda i, f: (0, 0)),
    ]
    out_specs = [pl.BlockSpec(memory_space=pl.ANY)]
    out_shape = [jax.ShapeDtypeStruct((m, d), F32)]
    args = [h, x, w_up, w_down, g_post]
    n_cast = 0
    if nxt is not None:
        g_next, next_layer, cast_ws = nxt
        n_cast = len(cast_ws)
        in_specs.append(pl.BlockSpec((1, d), lambda i, f: (0, 0)))
        args.append(g_next)
        out_specs.append(pl.BlockSpec(memory_space=pl.ANY))
        out_shape.append(jax.ShapeDtypeStruct((m, d), BF16))
        for w in cast_ws:
            _, r, c = w.shape
            br, bc = r // ni, c // nf
            assert br * ni == r and bc * nf == c and br % 16 == 0 and bc % LANES == 0
            in_specs.append(pl.BlockSpec((None, br, bc), lambda i, f: (next_layer, i, f)))
            args.append(w)
            out_specs.append(pl.BlockSpec((br, bc), lambda i, f: (i, f)))
            out_shape.append(jax.ShapeDtypeStruct((r, c), BF16))
            nbytes += 2 * br * bc * 6
    outs = pl.pallas_call(
        functools.partial(_mlp_kernel, n_cast=n_cast),
        grid=(ni, nf),
        in_specs=in_specs,
        out_specs=out_specs,
        out_shape=out_shape,
        scratch_shapes=_streamed_epilogue_scratch(tm, d, emit_h=nxt is not None),
        compiler_params=pltpu.CompilerParams(
            dimension_semantics=("arbitrary", "arbitrary"),
            vmem_limit_bytes=_vmem_limit(nbytes)),
        name="mlp",
    )(*args)
    if nxt is None:
        return outs[0], None, []
    return outs[0], outs[1], list(outs[2:])


def kernel(x, mem, g_pre_mix, w_in, conv_w, conv_b, ln_a_g, ln_a_b, ln_v_g, ln_v_b, w_spatial, b_spatial, w_out, g_post_mix, g_pre_xa, g_mem, w_q, w_kv, w_o, g_post_xa, g_pre_mlp, w_up, w_down, g_post_mlp):
    batch, seq, d = x.shape
    depth = w_in.shape[0]
    w_a = conv_w.shape[2]
    w_b = ln_v_g.shape[1]
    m = batch * seq

    restreamed = [w_out, w_o, w_up, w_down]
    w_out_b, w_o_b, w_up_b, w_down_b = [w[0].astype(BF16) for w in restreamed]

    xf = x.reshape(m, d)
    memf = mem.reshape(batch * N_MEM, d)

    def row(p, l):
        return p[l].reshape(1, -1)

    h = _prenorm(xf, row(g_pre_mix, 0), tm=512)
    for l in range(depth):
        hglu = _inproj_glu(h, w_in, l, w_a=w_a, tm=1024, tn=512)
        y_a = _conv_ln_silu(hglu, conv_w[l], row(conv_b, l), row(ln_a_g, l), row(ln_a_b, l),
                            batch=batch, seq=seq)
        y_b = _inproj_gmlp(h, w_in, row(ln_v_g, l), row(ln_v_b, l), w_spatial,
                           b_spatial[l].reshape(H_B, CHUNK, 1), l, w_a=w_a, w_b=w_b, tm=1024)
        xf, h = _proj_norm_res([y_a, y_b], w_out_b, xf, row(g_post_mix, l),
                               row(g_pre_xa, l), tm=512, tn=1024)
        kv = _norm_matmul(memf, row(g_mem, l), w_kv, l, tm=batch * N_MEM, tn=512,
                          out_dtype=BF16)
        o = _xattn(h, w_q, kv, l, seq=seq, tm=1024)
        xf, h = _proj_norm_res([o], w_o_b, xf, row(g_post_xa, l), row(g_pre_mlp, l),
                               tm=512, tn=1024)
        nxt = (row(g_pre_mix, l + 1), l + 1, restreamed) if l + 1 < depth else None
        xf, h, cast = _mlp(h, xf, w_up_b, w_down_b, row(g_post_mlp, l), nxt, tm=1024, tf=512)
        if cast:
            w_out_b, w_o_b, w_up_b, w_down_b = cast
    return xf.reshape(batch, seq, d)
```

```python
import functools

import jax
import jax.numpy as jnp
from jax import lax
from jax.experimental import pallas as pl
from jax.experimental.pallas import tpu as pltpu

F32 = jnp.float32
BF16 = jnp.bfloat16

EPS = 1e-6
H_A = 8
H_B = 8
CONV_K = 31
CHUNK = 128
N_MEM = 256
XA_HEADS = 4

V7X_VMEM_BYTES = 64 * 1024 * 1024
LANES = 128
SUBLANES = 8
CONV_PAD = 32
CONV_ROWS = 64
CONV_TAIL = CONV_ROWS + CONV_PAD
CONV_LN_ROWS = 256
NORM_ROWS = 64
NORM_COLS = 512


def _vmem_limit(nbytes):
    return int(min(nbytes + 12 * 1024 * 1024, V7X_VMEM_BYTES - 4 * 1024 * 1024))


def _lane_fold(v):
    acc = v[:, 0:LANES]
    for t in range(1, v.shape[1] // LANES):
        acc = acc + v[:, t * LANES:(t + 1) * LANES]
    return acc


def _row_rstd(src_ref, rows, d):
    part = None
    for s in range(d // NORM_COLS):
        a = src_ref[rows, s * NORM_COLS:(s + 1) * NORM_COLS]
        f = _lane_fold(a * a)
        part = f if part is None else part + f
    return lax.rsqrt(jnp.sum(part, axis=-1, keepdims=True) * (1.0 / d) + EPS)


def _scale_rows_to(src_ref, rows, rstd, g_ref, h_ref):
    d = src_ref.shape[1]
    for s in range(d // NORM_COLS):
        cols = slice(s * NORM_COLS, (s + 1) * NORM_COLS)
        h_ref[rows, cols] = (src_ref[rows, cols] * rstd * g_ref[:, cols]).astype(h_ref.dtype)


def _rmsnorm_rows_to(x_ref, g_ref, h_ref):
    tm, d = x_ref.shape

    def body(r, carry):
        rows = pl.ds(pl.multiple_of(r * NORM_ROWS, NORM_ROWS), NORM_ROWS)
        _scale_rows_to(x_ref, rows, _row_rstd(x_ref, rows, d), g_ref, h_ref)
        return carry

    lax.fori_loop(0, tm // NORM_ROWS, body, 0)


def _post_norm_residual_to_hbm(o_scr, x_scr, g_ref, gn_ref, h_scr, o_hbm, h_hbm, row0, sems):
    tm, d = o_scr.shape
    n = tm // NORM_ROWS

    def copies(r):
        rows = pl.ds(pl.multiple_of(r * NORM_ROWS, NORM_ROWS), NORM_ROWS)
        dst = pl.ds(pl.multiple_of(row0 + r * NORM_ROWS, NORM_ROWS), NORM_ROWS)
        return [pltpu.make_async_copy(o_scr.at[rows, :], o_hbm.at[dst, :], sems.at[0]),
                pltpu.make_async_copy(h_scr.at[rows, :], h_hbm.at[dst, :], sems.at[1])]

    def body(r, carry):
        rows = pl.ds(pl.multiple_of(r * NORM_ROWS, NORM_ROWS), NORM_ROWS)
        rstd = _row_rstd(o_scr, rows, d)
        part = None
        for s in range(d // NORM_COLS):
            cols = slice(s * NORM_COLS, (s + 1) * NORM_COLS)
            y = x_scr[rows, cols] + o_scr[rows, cols] * rstd * g_ref[:, cols]
            o_scr[rows, cols] = y
            f = _lane_fold(y * y)
            part = f if part is None else part + f
        rstd_n = lax.rsqrt(jnp.sum(part, axis=-1, keepdims=True) * (1.0 / d) + EPS)
        _scale_rows_to(o_scr, rows, rstd_n, gn_ref, h_scr)
        for c in copies(r):
            c.start()
        return carry

    lax.fori_loop(0, n, body, 0)

    def drain(r, carry):
        for c in copies(r):
            c.wait()
        return carry

    lax.fori_loop(0, n, drain, 0)


X_SLOTS = 4


class _EpilogueCopies:
    def __init__(self, o_scr, x_hbm, o_hbm, h_hbm, xbuf, hbuf, sems, row0):
        self.o_scr, self.x_hbm, self.o_hbm, self.h_hbm = o_scr, x_hbm, o_hbm, h_hbm
        self.xbuf, self.hbuf, self.sems, self.row0 = xbuf, hbuf, sems, row0

    def _hbm_rows(self, r):
        return pl.ds(pl.multiple_of(self.row0 + r * NORM_ROWS, NORM_ROWS), NORM_ROWS)

    def x_in(self, r, slot):
        return pltpu.make_async_copy(self.x_hbm.at[self._hbm_rows(r), :], self.xbuf.at[slot],
                                     self.sems.at[slot])

    def o_out(self, r):
        rows = pl.ds(pl.multiple_of(r * NORM_ROWS, NORM_ROWS), NORM_ROWS)
        return pltpu.make_async_copy(self.o_scr.at[rows, :], self.o_hbm.at[self._hbm_rows(r), :],
                                     self.sems.at[X_SLOTS])

    def h_out(self, r, slot):
        return pltpu.make_async_copy(self.hbuf.at[slot], self.h_hbm.at[self._hbm_rows(r), :],
                                     self.sems.at[X_SLOTS + 1 + slot])


def _post_norm_residual_streamed(cp, g_ref, gn_ref):
    o_scr, xbuf, hbuf = cp.o_scr, cp.xbuf, cp.hbuf
    tm, d = o_scr.shape
    n = tm // NORM_ROWS
    emit_h = cp.h_hbm is not None

    def body(r, carry):
        xslot = lax.rem(r, X_SLOTS)
        hslot = lax.rem(r, 2)
        rows = pl.ds(pl.multiple_of(r * NORM_ROWS, NORM_ROWS), NORM_ROWS)
        cp.x_in(r, xslot).wait()
        rstd = _row_rstd(o_scr, rows, d)
        part = None
        for s in range(d // NORM_COLS):
            cols = slice(s * NORM_COLS, (s + 1) * NORM_COLS)
            y = xbuf[xslot, :, cols] + o_scr[rows, cols] * rstd * g_ref[:, cols]
            o_scr[rows, cols] = y
            if emit_h:
                f = _lane_fold(y * y)
                part = f if part is None else part + f
        cp.o_out(r).start()

        @pl.when(r + X_SLOTS < n)
        def _():
            cp.x_in(r + X_SLOTS, xslot).start()

        if emit_h:
            rstd_n = lax.rsqrt(jnp.sum(part, axis=-1, keepdims=True) * (1.0 / d) + EPS)

            @pl.when(r >= 2)
            def _():
                cp.h_out(r - 2, hslot).wait()

            for s in range(d // NORM_COLS):
                cols = slice(s * NORM_COLS, (s + 1) * NORM_COLS)
                hbuf[hslot, :, cols] = (o_scr[rows, cols] * rstd_n * gn_ref[:, cols]
                                        ).astype(hbuf.dtype)
            cp.h_out(r, hslot).start()
        return carry

    lax.fori_loop(0, n, body, 0)

    def drain(r, carry):
        cp.o_out(r).wait()
        return carry

    lax.fori_loop(0, n, drain, 0)
    if emit_h:
        for r in (n - 2, n - 1):
            cp.h_out(r, r % 2).wait()


def _group_layernorm(v, g, b):
    mu = jnp.mean(v, axis=-1, keepdims=True)
    xc = v - mu
    var = jnp.mean(xc * xc, axis=-1, keepdims=True)
    return xc * lax.rsqrt(var + EPS) * g + b


def _prenorm_kernel(x_ref, g_ref, h_ref):
    _rmsnorm_rows_to(x_ref, g_ref, h_ref)


def _prenorm(x, g, *, tm):
    m, d = x.shape
    return pl.pallas_call(
        _prenorm_kernel,
        grid=(m // tm,),
        in_specs=[pl.BlockSpec((tm, d), lambda i: (i, 0)),
                  pl.BlockSpec((1, d), lambda i: (0, 0))],
        out_specs=pl.BlockSpec((tm, d), lambda i: (i, 0)),
        out_shape=jax.ShapeDtypeStruct((m, d), BF16),
        compiler_params=pltpu.CompilerParams(
            dimension_semantics=("arbitrary",),
            vmem_limit_bytes=_vmem_limit(2 * tm * d * 6)),
        name="prenorm",
    )(x, g)


def _norm_matmul_kernel(x_ref, g_ref, w_ref, o_ref, h_ref):
    @pl.when(pl.program_id(1) == 0)
    def _():
        _rmsnorm_rows_to(x_ref, g_ref, h_ref)

    o_ref[...] = jnp.dot(h_ref[...], w_ref[...].astype(BF16),
                         preferred_element_type=F32).astype(o_ref.dtype)


def _norm_matmul(x, g, w, layer, *, tm, tn, out_dtype):
    m, d = x.shape
    n = w.shape[2]
    nbytes = tm * d * 4 + tm * d * 2 + 2 * d * tn * 4 + d * tn * 2 + 2 * tm * tn * 4
    return pl.pallas_call(
        _norm_matmul_kernel,
        grid=(m // tm, n // tn),
        in_specs=[
            pl.BlockSpec((tm, d), lambda i, j: (i, 0), pipeline_mode=pl.Buffered(1)),
            pl.BlockSpec((1, d), lambda i, j: (0, 0)),
            pl.BlockSpec((None, d, tn), lambda i, j: (layer, 0, j)),
        ],
        out_specs=pl.BlockSpec((tm, tn), lambda i, j: (i, j)),
        out_shape=jax.ShapeDtypeStruct((m, n), out_dtype),
        scratch_shapes=[pltpu.VMEM((tm, d), BF16)],
        compiler_params=pltpu.CompilerParams(
            dimension_semantics=("arbitrary", "arbitrary"),
            vmem_limit_bytes=_vmem_limit(nbytes)),
        name="norm_matmul",
    )(x, g, w)


_W_ONCE = dict(pipeline_mode=pl.Buffered(1))


def _inproj_glu_kernel(h_ref, wa_ref, wg_ref, o_ref, wab_ref, wgb_ref):
    @pl.when(pl.program_id(1) == 0)
    def _():
        wab_ref[...] = wa_ref[...].astype(BF16)
        wgb_ref[...] = wg_ref[...].astype(BF16)

    h = h_ref[...]
    a = jnp.dot(h, wab_ref[...], preferred_element_type=F32)
    gate = jnp.dot(h, wgb_ref[...], preferred_element_type=F32)
    o_ref[...] = a * jax.nn.sigmoid(gate)


def _inproj_glu(h, w_in, layer, *, w_a, tm, tn):
    m, d = h.shape
    gate_off = w_a // tn
    nbytes = 2 * tm * d * 2 + 2 * d * tn * 4 + 2 * d * tn * 2 + 4 * tm * tn * 4
    return pl.pallas_call(
        _inproj_glu_kernel,
        grid=(w_a // tn, m // tm),
        in_specs=[
            pl.BlockSpec((tm, d), lambda j, i: (i, 0)),
            pl.BlockSpec((None, d, tn), lambda j, i: (layer, 0, j), **_W_ONCE),
            pl.BlockSpec((None, d, tn), lambda j, i: (layer, 0, j + gate_off), **_W_ONCE),
        ],
        out_specs=pl.BlockSpec((tm, tn), lambda j, i: (i, j)),
        out_shape=jax.ShapeDtypeStruct((m, w_a), F32),
        scratch_shapes=[pltpu.VMEM((d, tn), BF16), pltpu.VMEM((d, tn), BF16)],
        compiler_params=pltpu.CompilerParams(
            dimension_semantics=("arbitrary", "arbitrary"),
            vmem_limit_bytes=_vmem_limit(nbytes)),
        name="inproj_glu",
    )(h, w_in, w_in)


def _inproj_gmlp_kernel(h_ref, wu_ref, wv_ref, lg_ref, lb_ref, ws_ref, bs_ref, o_ref,
                        wub_ref, wvb_ref):
    @pl.when(pl.program_id(1) == 0)
    def _():
        wub_ref[...] = wu_ref[...].astype(BF16)
        wvb_ref[...] = wv_ref[...].astype(BF16)

    h = h_ref[...]
    u = jax.nn.gelu(jnp.dot(h, wub_ref[...], preferred_element_type=F32))
    v = jax.nn.gelu(jnp.dot(h, wvb_ref[...], preferred_element_type=F32))
    vn = _group_layernorm(v, lg_ref[...], lb_ref[...]).astype(BF16)

    t_idx = lax.broadcasted_iota(jnp.int32, (CHUNK, CHUNK), 0)
    s_idx = lax.broadcasted_iota(jnp.int32, (CHUNK, CHUNK), 1)
    w_sp = jnp.where(t_idx >= s_idx, ws_ref[...], 0.0).astype(BF16)
    bias = bs_ref[...]
    for c in range(o_ref.shape[0] // CHUNK):
        rows = slice(c * CHUNK, (c + 1) * CHUNK)
        s = jnp.dot(w_sp, vn[rows, :], preferred_element_type=F32)
        o_ref[rows, :] = (u[rows, :] * (s + bias)).astype(o_ref.dtype)


def _inproj_gmlp(h, w_in, ln_g, ln_b, w_sp, b_sp, layer, *, w_a, w_b, tm):
    m, d = h.shape
    d_b = w_b // H_B
    u_off = 2 * w_a // d_b
    v_off = u_off + H_B
    nbytes = (2 * tm * d * 2 + 4 * d * d_b * 4 + 2 * d * d_b * 2 + 2 * tm * d_b * 2
              + 4 * tm * d_b * 4)
    return pl.pallas_call(
        _inproj_gmlp_kernel,
        grid=(H_B, m // tm),
        in_specs=[
            pl.BlockSpec((tm, d), lambda j, i: (i, 0)),
            pl.BlockSpec((None, d, d_b), lambda j, i: (layer, 0, j + u_off)),
            pl.BlockSpec((None, d, d_b), lambda j, i: (layer, 0, j + v_off)),
            pl.BlockSpec((1, d_b), lambda j, i: (0, j)),
            pl.BlockSpec((1, d_b), lambda j, i: (0, j)),
            pl.BlockSpec((None, None, CHUNK, CHUNK), lambda j, i: (layer, j, 0, 0)),
            pl.BlockSpec((None, CHUNK, 1), lambda j, i: (j, 0, 0)),
        ],
        out_specs=pl.BlockSpec((tm, d_b), lambda j, i: (i, j)),
        out_shape=jax.ShapeDtypeStruct((m, w_b), BF16),
        scratch_shapes=[pltpu.VMEM((d, d_b), BF16), pltpu.VMEM((d, d_b), BF16)],
        compiler_params=pltpu.CompilerParams(
            dimension_semantics=("arbitrary", "arbitrary"),
            vmem_limit_bytes=_vmem_limit(nbytes)),
        name="inproj_gmlp",
    )(h, w_in, w_in, ln_g, ln_b, w_sp, b_sp)


def _conv_ln_silu_kernel(x_ref, w_ref, cb_ref, lg_ref, lb_ref, o_ref, p_ref, a_ref):
    seq, c = x_ref.shape
    rows = CONV_ROWS
    p_ref[0, 0:CONV_PAD, :] = jnp.zeros((CONV_PAD, c), F32)
    p_ref[0, CONV_PAD + seq:, :] = jnp.zeros((CONV_TAIL, c), F32)
    p_ref[0, CONV_PAD:CONV_PAD + seq, :] = x_ref[...]

    def build(i, carry):
        t0 = pl.multiple_of(i * rows, rows)
        win = p_ref[0, pl.ds(t0, rows + SUBLANES), :]
        for r in range(1, SUBLANES):
            p_ref[r, pl.ds(t0, rows), :] = win[r:r + rows, :]
        return carry

    lax.fori_loop(0, (seq + rows) // rows, build, 0)
    shift = CONV_PAD - (CONV_K - 1)

    def taps(i, carry):
        t0 = pl.multiple_of(i * rows, rows)
        acc = jnp.broadcast_to(cb_ref[...], (rows, c))
        for k in range(CONV_K):
            q, r = divmod(shift + k, SUBLANES)
            start = pl.multiple_of(t0 + SUBLANES * q, SUBLANES)
            acc = acc + w_ref[k:k + 1, :] * p_ref[r, pl.ds(start, rows), :]
        a_ref[pl.ds(t0, rows), :] = acc
        return carry

    lax.fori_loop(0, seq // rows, taps, 0)

    def finish(i, carry):
        t0 = pl.multiple_of(i * CONV_LN_ROWS, CONV_LN_ROWS)
        y = _group_layernorm(a_ref[pl.ds(t0, CONV_LN_ROWS), :], lg_ref[...], lb_ref[...])
        o_ref[pl.ds(t0, CONV_LN_ROWS), :] = (y * jax.nn.sigmoid(y)).astype(o_ref.dtype)
        return carry

    lax.fori_loop(0, seq // CONV_LN_ROWS, finish, 0)


def _conv_ln_silu(hglu, conv_w, conv_b, ln_g, ln_b, *, batch, seq):
    m, w_a = hglu.shape
    cg = w_a // H_A
    p_rows = CONV_PAD + seq + CONV_TAIL
    return pl.pallas_call(
        _conv_ln_silu_kernel,
        grid=(batch, H_A),
        in_specs=[
            pl.BlockSpec((seq, cg), lambda b, c: (b, c)),
            pl.BlockSpec((CONV_K, cg), lambda b, c: (0, c)),
            pl.BlockSpec((1, cg), lambda b, c: (0, c)),
            pl.BlockSpec((1, cg), lambda b, c: (0, c)),
            pl.BlockSpec((1, cg), lambda b, c: (0, c)),
        ],
        out_specs=pl.BlockSpec((seq, cg), lambda b, c: (b, c)),
        out_shape=jax.ShapeDtypeStruct((m, w_a), BF16),
        scratch_shapes=[pltpu.VMEM((SUBLANES, p_rows, cg), F32), pltpu.VMEM((seq, cg), F32)],
        compiler_params=pltpu.CompilerParams(
            dimension_semantics=("arbitrary", "arbitrary"),
            vmem_limit_bytes=_vmem_limit(
                SUBLANES * p_rows * cg * 4 + seq * cg * 4 + 2 * seq * cg * 6)),
        name="conv_ln_silu",
    )(hglu, conv_w, conv_b, ln_g, ln_b)


def _proj_norm_res_kernel(*refs, n_a):
    a_refs = refs[:n_a]
    (w_ref, x_ref, g_ref, gn_ref, o_hbm, h_hbm, o_scr, x_scr, h_scr, sems) = refs[n_a:]
    i = pl.program_id(0)
    j = pl.program_id(1)
    tm, tn = x_ref.shape
    acc = None
    off = 0
    for a_ref in a_refs:
        ka = a_ref.shape[1]
        part = jnp.dot(a_ref[...], w_ref[off:off + ka, :], preferred_element_type=F32)
        acc = part if acc is None else acc + part
        off += ka
    cols = pl.ds(pl.multiple_of(j * tn, tn), tn)
    o_scr[:, cols] = acc
    x_scr[:, cols] = x_ref[...]

    @pl.when(j == pl.num_programs(1) - 1)
    def _():
        _post_norm_residual_to_hbm(o_scr, x_scr, g_ref, gn_ref, h_scr, o_hbm, h_hbm,
                                   i * tm, sems)


def _proj_norm_res(a_list, w, x, g, g_next, *, tm, tn):
    m, d = x.shape
    k = w.shape[0]
    n_a = len(a_list)
    nbytes = 2 * tm * k * 2 + 2 * k * tn * 2 + 2 * tm * tn * 4 + 2 * tm * d * 4 + tm * d * 2
    a_specs = [pl.BlockSpec((tm, a.shape[1]), lambda i, j: (i, 0)) for a in a_list]
    return pl.pallas_call(
        functools.partial(_proj_norm_res_kernel, n_a=n_a),
        grid=(m // tm, d // tn),
        in_specs=a_specs + [
            pl.BlockSpec((k, tn), lambda i, j: (0, j)),
            pl.BlockSpec((tm, tn), lambda i, j: (i, j)),
            pl.BlockSpec((1, d), lambda i, j: (0, 0)),
            pl.BlockSpec((1, d), lambda i, j: (0, 0)),
        ],
        out_specs=[pl.BlockSpec(memory_space=pl.ANY), pl.BlockSpec(memory_space=pl.ANY)],
        out_shape=[jax.ShapeDtypeStruct((m, d), F32), jax.ShapeDtypeStruct((m, d), BF16)],
        scratch_shapes=[pltpu.VMEM((tm, d), F32), pltpu.VMEM((tm, d), F32),
                        pltpu.VMEM((tm, d), BF16), pltpu.SemaphoreType.DMA((2,))],
        compiler_params=pltpu.CompilerParams(
            dimension_semantics=("arbitrary", "arbitrary"),
            vmem_limit_bytes=_vmem_limit(nbytes)),
        name="proj_norm_res",
    )(*a_list, w, x, g, g_next)


def _xattn_kernel(h_ref, wq_ref, k_ref, v_ref, o_ref, wqb_ref, *, scale):
    @pl.when(pl.program_id(1) == 0)
    def _():
        wqb_ref[...] = wq_ref[...].astype(BF16)

    q = jnp.dot(h_ref[...], wqb_ref[...], preferred_element_type=F32).astype(BF16)
    s = lax.dot_general(q, k_ref[...], (((1,), (1,)), ((), ())),
                        preferred_element_type=F32) * scale
    p = jnp.exp(s - jnp.max(s, axis=-1, keepdims=True))
    p = (p / jnp.sum(p, axis=-1, keepdims=True)).astype(BF16)
    o_ref[...] = jnp.dot(p, v_ref[...], preferred_element_type=F32).astype(o_ref.dtype)


def _xattn(h, w_q, kv, layer, *, seq, tm):
    m, d = h.shape
    dh = d // XA_HEADS
    nbytes = (2 * tm * d * 2 + d * dh * 4 + d * dh * 2 + 8 * N_MEM * dh * 2 + 2 * tm * dh * 2
              + tm * dh * 6)
    return pl.pallas_call(
        functools.partial(_xattn_kernel, scale=float(dh) ** -0.5),
        grid=(XA_HEADS, m // tm),
        in_specs=[
            pl.BlockSpec((tm, d), lambda j, i: (i, 0)),
            pl.BlockSpec((None, d, dh), lambda j, i: (layer, 0, j), **_W_ONCE),
            pl.BlockSpec((N_MEM, dh), lambda j, i: ((i * tm) // seq, j)),
            pl.BlockSpec((N_MEM, dh), lambda j, i: ((i * tm) // seq, j + XA_HEADS)),
        ],
        out_specs=pl.BlockSpec((tm, dh), lambda j, i: (i, j)),
        out_shape=jax.ShapeDtypeStruct((m, d), BF16),
        scratch_shapes=[pltpu.VMEM((d, dh), BF16)],
        compiler_params=pltpu.CompilerParams(
            dimension_semantics=("arbitrary", "arbitrary"),
            vmem_limit_bytes=_vmem_limit(nbytes)),
        name="xattn",
    )(h, w_q, kv, kv)


def _mlp_kernel(*refs, n_cast):
    if n_cast:
        h_ref, x_hbm, wu_ref, wd_ref, g_ref, gn_ref = refs[:6]
        cast_in = refs[6:6 + n_cast]
        o_hbm, hn_hbm = refs[6 + n_cast:8 + n_cast]
        cast_out = refs[8 + n_cast:8 + 2 * n_cast]
        o_scr, xbuf, hbuf, sems = refs[8 + 2 * n_cast:]
    else:
        (h_ref, x_hbm, wu_ref, wd_ref, g_ref, o_hbm, o_scr, xbuf, sems) = refs
        gn_ref = hn_hbm = hbuf = None
        cast_in = cast_out = ()
    i = pl.program_id(0)
    f = pl.program_id(1)
    last = pl.num_programs(1) - 1
    cp = _EpilogueCopies(o_scr, x_hbm, o_hbm, hn_hbm, xbuf, hbuf, sems, i * o_scr.shape[0])

    @pl.when(f == 0)
    def _():
        o_scr[...] = jnp.zeros(o_scr.shape, F32)

    @pl.when(f == last)
    def _():
        for r in range(X_SLOTS):
            cp.x_in(r, r).start()

    for src, dst in zip(cast_in, cast_out):
        dst[...] = src[...].astype(dst.dtype)
    u = jnp.dot(h_ref[...], wu_ref[...], preferred_element_type=F32)
    u = jnp.square(jnp.maximum(u, 0.0)).astype(BF16)
    o_scr[...] += jnp.dot(u, wd_ref[...], preferred_element_type=F32)

    @pl.when(f == last)
    def _():
        _post_norm_residual_streamed(cp, g_ref, gn_ref)


def _streamed_epilogue_scratch(tm, d, emit_h):
    scratch = [pltpu.VMEM((tm, d), F32), pltpu.VMEM((X_SLOTS, NORM_ROWS, d), F32)]
    if emit_h:
        scratch.append(pltpu.VMEM((2, NORM_ROWS, d), BF16))
    return scratch + [pltpu.SemaphoreType.DMA((X_SLOTS + 3,))]


def _mlp(h, x, w_up, w_down, g_post, nxt, *, tm, tf):
    m, d = x.shape
    d_ff = w_up.shape[1]
    ni, nf = m // tm, d_ff // tf
    nbytes = ((2 if nxt is None else 1) * tm * d * 2 + 4 * d * tf * 2 + tm * d * 4
              + (X_SLOTS + 1) * NORM_ROWS * d * 4 + tm * tf * 6)
    h_buffers = pl.Buffered(2 if nxt is None else 1)
    in_specs = [
        pl.BlockSpec((tm, d), lambda i, f: (i, 0), pipeline_mode=h_buffers),
        pl.BlockSpec(memory_space=pl.ANY),
        pl.BlockSpec((d, tf), lambda i, f: (0, f)),
        pl.BlockSpec((tf, d), lambda i, f: (f, 0)),
        pl.BlockSpec((1, d), lambda i, f: (0, 0)),
    ]
    out_specs = [pl.BlockSpec(memory_space=pl.ANY)]
    out_shape = [jax.ShapeDtypeStruct((m, d), F32)]
    args = [h, x, w_up, w_down, g_post]
    n_cast = 0
    if nxt is not None:
        g_next, next_layer, cast_ws = nxt
        n_cast = len(cast_ws)
        in_specs.append(pl.BlockSpec((1, d), lambda i, f: (0, 0)))
        args.append(g_next)
        out_specs.append(pl.BlockSpec(memory_space=pl.ANY))
        out_shape.append(jax.ShapeDtypeStruct((m, d), BF16))
        for w in cast_ws:
            _, r, c = w.shape
            br, bc = r // ni, c // nf
            assert br * ni == r and bc * nf == c and br % 16 == 0 and bc % LANES == 0
            in_specs.append(pl.BlockSpec((None, br, bc), lambda i, f: (next_layer, i, f)))
            args.append(w)
            out_specs.append(pl.BlockSpec((br, bc), lambda i, f: (i, f)))
            out_shape.append(jax.ShapeDtypeStruct((r, c), BF16))
            nbytes += 2 * br * bc * 6
    outs = pl.pallas_call(
        functools.partial(_mlp_kernel, n_cast=n_cast),
        grid=(ni, nf),
        in_specs=in_specs,
        out_specs=out_specs,
        out_shape=out_shape,
        scratch_shapes=_streamed_epilogue_scratch(tm, d, emit_h=nxt is not None),
        compiler_params=pltpu.CompilerParams(
            dimension_semantics=("arbitrary", "arbitrary"),
            vmem_limit_bytes=_vmem_limit(nbytes)),
        name="mlp",
    )(*args)
    if nxt is None:
        return outs[0], None, []
    return outs[0], outs[1], list(outs[2:])


def kernel(x, mem, g_pre_mix, w_in, conv_w, conv_b, ln_a_g, ln_a_b, ln_v_g, ln_v_b, w_spatial, b_spatial, w_out, g_post_mix, g_pre_xa, g_mem, w_q, w_kv, w_o, g_post_xa, g_pre_mlp, w_up, w_down, g_post_mlp):
    batch, seq, d = x.shape
    depth = w_in.shape[0]
    w_a = conv_w.shape[2]
    w_b = ln_v_g.shape[1]
    m = batch * seq

    restreamed = [w_out, w_o, w_up, w_down]
    w_out_b, w_o_b, w_up_b, w_down_b = [w[0].astype(BF16) for w in restreamed]

    xf = x.reshape(m, d)
    memf = mem.reshape(batch * N_MEM, d)

    def row(p, l):
        return p[l].reshape(1, -1)

    h = _prenorm(xf, row(g_pre_mix, 0), tm=512)
    for l in range(depth):
        hglu = _inproj_glu(h, w_in, l, w_a=w_a, tm=1024, tn=512)
        y_a = _conv_ln_silu(hglu, conv_w[l], row(conv_b, l), row(ln_a_g, l), row(ln_a_b, l),
                            batch=batch, seq=seq)
        y_b = _inproj_gmlp(h, w_in, row(ln_v_g, l), row(ln_v_b, l), w_spatial,
                           b_spatial[l].reshape(H_B, CHUNK, 1), l, w_a=w_a, w_b=w_b, tm=1024)
        xf, h = _proj_norm_res([y_a, y_b], w_out_b, xf, row(g_post_mix, l),
                               row(g_pre_xa, l), tm=512, tn=1024)
        kv = _norm_matmul(memf, row(g_mem, l), w_kv, l, tm=batch * N_MEM, tn=512,
                          out_dtype=BF16)
        o = _xattn(h, w_q, kv, l, seq=seq, tm=1024)
        xf, h = _proj_norm_res([o], w_o_b, xf, row(g_post_xa, l), row(g_pre_mlp, l),
                               tm=512, tn=1024)
        nxt = (row(g_pre_mix, l + 1), l + 1, restreamed) if l + 1 < depth else None
        xf, h, cast = _mlp(h, xf, w_up_b, w_down_b, row(g_post_mlp, l), nxt, tm=1024, tf=512)
        if cast:
            w_out_b, w_o_b, w_up_b, w_down_b = cast
    return xf.reshape(batch, seq, d)
```

```python
import functools

import jax
import jax.numpy as jnp
from jax import lax
from jax.experimental import pallas as pl
from jax.experimental.pallas import tpu as pltpu

F32 = jnp.float32
BF16 = jnp.bfloat16

EPS = 1e-6
H_A = 8
H_B = 8
CONV_K = 31
CHUNK = 128
N_MEM = 256
XA_HEADS = 4

V7X_VMEM_BYTES = 64 * 1024 * 1024
LANES = 128
SUBLANES = 8
CONV_PAD = 32
CONV_ROWS = 64
CONV_TAIL = CONV_ROWS + CONV_PAD
CONV_LN_ROWS = 512
NORM_ROWS = 64
NORM_COLS = 512


def _vmem_limit(nbytes):
    return int(min(nbytes + 12 * 1024 * 1024, V7X_VMEM_BYTES - 4 * 1024 * 1024))


def _lane_fold(v):
    acc = v[:, 0:LANES]
    for t in range(1, v.shape[1] // LANES):
        acc = acc + v[:, t * LANES:(t + 1) * LANES]
    return acc


def _row_rstd(src_ref, rows, d):
    part = None
    for s in range(d // NORM_COLS):
        a = src_ref[rows, s * NORM_COLS:(s + 1) * NORM_COLS]
        f = _lane_fold(a * a)
        part = f if part is None else part + f
    return lax.rsqrt(jnp.sum(part, axis=-1, keepdims=True) * (1.0 / d) + EPS)


def _scale_rows_to(src_ref, rows, rstd, g_ref, h_ref):
    d = src_ref.shape[1]
    for s in range(d // NORM_COLS):
        cols = slice(s * NORM_COLS, (s + 1) * NORM_COLS)
        h_ref[rows, cols] = (src_ref[rows, cols] * rstd * g_ref[:, cols]).astype(h_ref.dtype)


def _rmsnorm_rows_to(x_ref, g_ref, h_ref):
    tm, d = x_ref.shape

    def body(r, carry):
        rows = pl.ds(pl.multiple_of(r * NORM_ROWS, NORM_ROWS), NORM_ROWS)
        _scale_rows_to(x_ref, rows, _row_rstd(x_ref, rows, d), g_ref, h_ref)
        return carry

    lax.fori_loop(0, tm // NORM_ROWS, body, 0)


def _post_norm_residual_to_hbm(o_scr, x_scr, g_ref, gn_ref, h_scr, o_hbm, h_hbm, row0, sems):
    tm, d = o_scr.shape
    n = tm // NORM_ROWS

    def copies(r):
        rows = pl.ds(pl.multiple_of(r * NORM_ROWS, NORM_ROWS), NORM_ROWS)
        dst = pl.ds(pl.multiple_of(row0 + r * NORM_ROWS, NORM_ROWS), NORM_ROWS)
        return [pltpu.make_async_copy(o_scr.at[rows, :], o_hbm.at[dst, :], sems.at[0]),
                pltpu.make_async_copy(h_scr.at[rows, :], h_hbm.at[dst, :], sems.at[1])]

    def body(r, carry):
        rows = pl.ds(pl.multiple_of(r * NORM_ROWS, NORM_ROWS), NORM_ROWS)
        rstd = _row_rstd(o_scr, rows, d)
        part = None
        for s in range(d // NORM_COLS):
            cols = slice(s * NORM_COLS, (s + 1) * NORM_COLS)
            y = x_scr[rows, cols] + o_scr[rows, cols] * rstd * g_ref[:, cols]
            o_scr[rows, cols] = y
            f = _lane_fold(y * y)
            part = f if part is None else part + f
        rstd_n = lax.rsqrt(jnp.sum(part, axis=-1, keepdims=True) * (1.0 / d) + EPS)
        _scale_rows_to(o_scr, rows, rstd_n, gn_ref, h_scr)
        for c in copies(r):
            c.start()
        return carry

    lax.fori_loop(0, n, body, 0)

    def drain(r, carry):
        for c in copies(r):
            c.wait()
        return carry

    lax.fori_loop(0, n, drain, 0)


X_SLOTS = 4


class _EpilogueCopies:
    def __init__(self, o_scr, x_hbm, o_hbm, h_hbm, xbuf, hbuf, sems, row0):
        self.o_scr, self.x_hbm, self.o_hbm, self.h_hbm = o_scr, x_hbm, o_hbm, h_hbm
        self.xbuf, self.hbuf, self.sems, self.row0 = xbuf, hbuf, sems, row0

    def _hbm_rows(self, r):
        return pl.ds(pl.multiple_of(self.row0 + r * NORM_ROWS, NORM_ROWS), NORM_ROWS)

    def x_in(self, r, slot):
        return pltpu.make_async_copy(self.x_hbm.at[self._hbm_rows(r), :], self.xbuf.at[slot],
                                     self.sems.at[slot])

    def o_out(self, r):
        rows = pl.ds(pl.multiple_of(r * NORM_ROWS, NORM_ROWS), NORM_ROWS)
        return pltpu.make_async_copy(self.o_scr.at[rows, :], self.o_hbm.at[self._hbm_rows(r), :],
                                     self.sems.at[X_SLOTS])

    def h_out(self, r, slot):
        return pltpu.make_async_copy(self.hbuf.at[slot], self.h_hbm.at[self._hbm_rows(r), :],
                                     self.sems.at[X_SLOTS + 1 + slot])


def _post_norm_residual_streamed(cp, g_ref, gn_ref):
    o_scr, xbuf, hbuf = cp.o_scr, cp.xbuf, cp.hbuf
    tm, d = o_scr.shape
    n = tm // NORM_ROWS
    emit_h = cp.h_hbm is not None

    def body(r, carry):
        xslot = lax.rem(r, X_SLOTS)
        hslot = lax.rem(r, 2)
        rows = pl.ds(pl.multiple_of(r * NORM_ROWS, NORM_ROWS), NORM_ROWS)
        cp.x_in(r, xslot).wait()
        rstd = _row_rstd(o_scr, rows, d)
        part = None
        for s in range(d // NORM_COLS):
            cols = slice(s * NORM_COLS, (s + 1) * NORM_COLS)
            y = xbuf[xslot, :, cols] + o_scr[rows, cols] * rstd * g_ref[:, cols]
            o_scr[rows, cols] = y
            if emit_h:
                f = _lane_fold(y * y)
                part = f if part is None else part + f
        cp.o_out(r).start()

        @pl.when(r + X_SLOTS < n)
        def _():
            cp.x_in(r + X_SLOTS, xslot).start()

        if emit_h:
            rstd_n = lax.rsqrt(jnp.sum(part, axis=-1, keepdims=True) * (1.0 / d) + EPS)

            @pl.when(r >= 2)
            def _():
                cp.h_out(r - 2, hslot).wait()

            for s in range(d // NORM_COLS):
                cols = slice(s * NORM_COLS, (s + 1) * NORM_COLS)
                hbuf[hslot, :, cols] = (o_scr[rows, cols] * rstd_n * gn_ref[:, cols]
                                        ).astype(hbuf.dtype)
            cp.h_out(r, hslot).start()
        return carry

    lax.fori_loop(0, n, body, 0)

    def drain(r, carry):
        cp.o_out(r).wait()
        return carry

    lax.fori_loop(0, n, drain, 0)
    if emit_h:
        for r in (n - 2, n - 1):
            cp.h_out(r, r % 2).wait()


def _group_layernorm(v, g, b):
    mu = jnp.mean(v, axis=-1, keepdims=True)
    xc = v - mu
    var = jnp.mean(xc * xc, axis=-1, keepdims=True)
    return xc * lax.rsqrt(var + EPS) * g + b


def _prenorm_kernel(x_ref, g_ref, h_ref):
    _rmsnorm_rows_to(x_ref, g_ref, h_ref)


def _prenorm(x, g, *, tm):
    m, d = x.shape
    return pl.pallas_call(
        _prenorm_kernel,
        grid=(m // tm,),
        in_specs=[pl.BlockSpec((tm, d), lambda i: (i, 0)),
                  pl.BlockSpec((1, d), lambda i: (0, 0))],
        out_specs=pl.BlockSpec((tm, d), lambda i: (i, 0)),
        out_shape=jax.ShapeDtypeStruct((m, d), BF16),
        compiler_params=pltpu.CompilerParams(
            dimension_semantics=("arbitrary",),
            vmem_limit_bytes=_vmem_limit(2 * tm * d * 6)),
        name="prenorm",
    )(x, g)


def _norm_matmul_kernel(x_ref, g_ref, w_ref, o_ref, h_ref):
    @pl.when(pl.program_id(1) == 0)
    def _():
        _rmsnorm_rows_to(x_ref, g_ref, h_ref)

    o_ref[...] = jnp.dot(h_ref[...], w_ref[...].astype(BF16),
                         preferred_element_type=F32).astype(o_ref.dtype)


def _norm_matmul(x, g, w, layer, *, tm, tn, out_dtype):
    m, d = x.shape
    n = w.shape[2]
    nbytes = tm * d * 4 + tm * d * 2 + 2 * d * tn * 4 + d * tn * 2 + 2 * tm * tn * 4
    return pl.pallas_call(
        _norm_matmul_kernel,
        grid=(m // tm, n // tn),
        in_specs=[
            pl.BlockSpec((tm, d), lambda i, j: (i, 0), pipeline_mode=pl.Buffered(1)),
            pl.BlockSpec((1, d), lambda i, j: (0, 0)),
            pl.BlockSpec((None, d, tn), lambda i, j: (layer, 0, j)),
        ],
        out_specs=pl.BlockSpec((tm, tn), lambda i, j: (i, j)),
        out_shape=jax.ShapeDtypeStruct((m, n), out_dtype),
        scratch_shapes=[pltpu.VMEM((tm, d), BF16)],
        compiler_params=pltpu.CompilerParams(
            dimension_semantics=("arbitrary", "arbitrary"),
            vmem_limit_bytes=_vmem_limit(nbytes)),
        name="norm_matmul",
    )(x, g, w)


_W_ONCE = dict(pipeline_mode=pl.Buffered(1))


def _inproj_glu_kernel(h_ref, wa_ref, wg_ref, o_ref, wab_ref, wgb_ref):
    @pl.when(pl.program_id(1) == 0)
    def _():
        wab_ref[...] = wa_ref[...].astype(BF16)
        wgb_ref[...] = wg_ref[...].astype(BF16)

    h = h_ref[...]
    a = jnp.dot(h, wab_ref[...], preferred_element_type=F32)
    gate = jnp.dot(h, wgb_ref[...], preferred_element_type=F32)
    o_ref[...] = a * jax.nn.sigmoid(gate)


def _inproj_glu(h, w_in, layer, *, w_a, tm, tn):
    m, d = h.shape
    gate_off = w_a // tn
    nbytes = 2 * tm * d * 2 + 2 * d * tn * 4 + 2 * d * tn * 2 + 4 * tm * tn * 4
    return pl.pallas_call(
        _inproj_glu_kernel,
        grid=(w_a // tn, m // tm),
        in_specs=[
            pl.BlockSpec((tm, d), lambda j, i: (i, 0)),
            pl.BlockSpec((None, d, tn), lambda j, i: (layer, 0, j), **_W_ONCE),
            pl.BlockSpec((None, d, tn), lambda j, i: (layer, 0, j + gate_off), **_W_ONCE),
        ],
        out_specs=pl.BlockSpec((tm, tn), lambda j, i: (i, j)),
        out_shape=jax.ShapeDtypeStruct((m, w_a), F32),
        scratch_shapes=[pltpu.VMEM((d, tn), BF16), pltpu.VMEM((d, tn), BF16)],
        compiler_params=pltpu.CompilerParams(
            dimension_semantics=("arbitrary", "arbitrary"),
            vmem_limit_bytes=_vmem_limit(nbytes)),
        name="inproj_glu",
    )(h, w_in, w_in)


def _inproj_gmlp_kernel(h_ref, wu_ref, wv_ref, lg_ref, lb_ref, ws_ref, bs_ref, o_ref,
                        wub_ref, wvb_ref):
    @pl.when(pl.program_id(1) == 0)
    def _():
        wub_ref[...] = wu_ref[...].astype(BF16)
        wvb_ref[...] = wv_ref[...].astype(BF16)

    h = h_ref[...]
    u = jax.nn.gelu(jnp.dot(h, wub_ref[...], preferred_element_type=F32))
    v = jax.nn.gelu(jnp.dot(h, wvb_ref[...], preferred_element_type=F32))
    vn = _group_layernorm(v, lg_ref[...], lb_ref[...]).astype(BF16)

    t_idx = lax.broadcasted_iota(jnp.int32, (CHUNK, CHUNK), 0)
    s_idx = lax.broadcasted_iota(jnp.int32, (CHUNK, CHUNK), 1)
    w_sp = jnp.where(t_idx >= s_idx, ws_ref[...], 0.0).astype(BF16)
    bias = bs_ref[...]
    for c in range(o_ref.shape[0] // CHUNK):
        rows = slice(c * CHUNK, (c + 1) * CHUNK)
        s = jnp.dot(w_sp, vn[rows, :], preferred_element_type=F32)
        o_ref[rows, :] = (u[rows, :] * (s + bias)).astype(o_ref.dtype)


def _inproj_gmlp(h, w_in, ln_g, ln_b, w_sp, b_sp, layer, *, w_a, w_b, tm):
    m, d = h.shape
    d_b = w_b // H_B
    u_off = 2 * w_a // d_b
    v_off = u_off + H_B
    nbytes = (2 * tm * d * 2 + 4 * d * d_b * 4 + 2 * d * d_b * 2 + 2 * tm * d_b * 2
              + 4 * tm * d_b * 4)
    return pl.pallas_call(
        _inproj_gmlp_kernel,
        grid=(H_B, m // tm),
        in_specs=[
            pl.BlockSpec((tm, d), lambda j, i: (i, 0)),
            pl.BlockSpec((None, d, d_b), lambda j, i: (layer, 0, j + u_off)),
            pl.BlockSpec((None, d, d_b), lambda j, i: (layer, 0, j + v_off)),
            pl.BlockSpec((1, d_b), lambda j, i: (0, j)),
            pl.BlockSpec((1, d_b), lambda j, i: (0, j)),
            pl.BlockSpec((None, None, CHUNK, CHUNK), lambda j, i: (layer, j, 0, 0)),
            pl.BlockSpec((None, CHUNK, 1), lambda j, i: (j, 0, 0)),
        ],
        out_specs=pl.BlockSpec((tm, d_b), lambda j, i: (i, j)),
        out_shape=jax.ShapeDtypeStruct((m, w_b), BF16),
        scratch_shapes=[pltpu.VMEM((d, d_b), BF16), pltpu.VMEM((d, d_b), BF16)],
        compiler_params=pltpu.CompilerParams(
            dimension_semantics=("arbitrary", "arbitrary"),
            vmem_limit_bytes=_vmem_limit(nbytes)),
        name="inproj_gmlp",
    )(h, w_in, w_in, ln_g, ln_b, w_sp, b_sp)


def _conv_ln_silu_kernel(x_ref, w_ref, cb_ref, lg_ref, lb_ref, o_ref, p_ref, a_ref):
    seq, c = x_ref.shape
    rows = CONV_ROWS
    p_ref[0, 0:CONV_PAD, :] = jnp.zeros((CONV_PAD, c), F32)
    p_ref[0, CONV_PAD + seq:, :] = jnp.zeros((CONV_TAIL, c), F32)
    p_ref[0, CONV_PAD:CONV_PAD + seq, :] = x_ref[...]

    def build(i, carry):
        t0 = pl.multiple_of(i * rows, rows)
        win = p_ref[0, pl.ds(t0, rows + SUBLANES), :]
        for r in range(1, SUBLANES):
            p_ref[r, pl.ds(t0, rows), :] = win[r:r + rows, :]
        return carry

    lax.fori_loop(0, (seq + rows) // rows, build, 0)
    shift = CONV_PAD - (CONV_K - 1)

    def taps(i, carry):
        t0 = pl.multiple_of(i * rows, rows)
        acc = jnp.broadcast_to(cb_ref[...], (rows, c))
        for k in range(CONV_K):
            q, r = divmod(shift + k, SUBLANES)
            start = pl.multiple_of(t0 + SUBLANES * q, SUBLANES)
            acc = acc + w_ref[k:k + 1, :] * p_ref[r, pl.ds(start, rows), :]
        a_ref[pl.ds(t0, rows), :] = acc
        return carry

    lax.fori_loop(0, seq // rows, taps, 0)

    def finish(i, carry):
        t0 = pl.multiple_of(i * CONV_LN_ROWS, CONV_LN_ROWS)
        y = _group_layernorm(a_ref[pl.ds(t0, CONV_LN_ROWS), :], lg_ref[...], lb_ref[...])
        o_ref[pl.ds(t0, CONV_LN_ROWS), :] = (y * jax.nn.sigmoid(y)).astype(o_ref.dtype)
        return carry

    lax.fori_loop(0, seq // CONV_LN_ROWS, finish, 0)


def _conv_ln_silu(hglu, conv_w, conv_b, ln_g, ln_b, *, batch, seq):
    m, w_a = hglu.shape
    cg = w_a // H_A
    p_rows = CONV_PAD + seq + CONV_TAIL
    return pl.pallas_call(
        _conv_ln_silu_kernel,
        grid=(batch, H_A),
        in_specs=[
            pl.BlockSpec((seq, cg), lambda b, c: (b, c)),
            pl.BlockSpec((CONV_K, cg), lambda b, c: (0, c)),
            pl.BlockSpec((1, cg), lambda b, c: (0, c)),
            pl.BlockSpec((1, cg), lambda b, c: (0, c)),
            pl.BlockSpec((1, cg), lambda b, c: (0, c)),
        ],
        out_specs=pl.BlockSpec((seq, cg), lambda b, c: (b, c)),
        out_shape=jax.ShapeDtypeStruct((m, w_a), BF16),
        scratch_shapes=[pltpu.VMEM((SUBLANES, p_rows, cg), F32), pltpu.VMEM((seq, cg), F32)],
        compiler_params=pltpu.CompilerParams(
            dimension_semantics=("arbitrary", "arbitrary"),
            vmem_limit_bytes=_vmem_limit(
                SUBLANES * p_rows * cg * 4 + seq * cg * 4 + 2 * seq * cg * 6)),
        name="conv_ln_silu",
    )(hglu, conv_w, conv_b, ln_g, ln_b)


def _proj_norm_res_kernel(*refs, n_a):
    a_refs = refs[:n_a]
    (w_ref, x_ref, g_ref, gn_ref, o_hbm, h_hbm, o_scr, x_scr, h_scr, sems) = refs[n_a:]
    i = pl.program_id(0)
    j = pl.program_id(1)
    tm, tn = x_ref.shape
    acc = None
    off = 0
    for a_ref in a_refs:
        ka = a_ref.shape[1]
        part = jnp.dot(a_ref[...], w_ref[off:off + ka, :], preferred_element_type=F32)
        acc = part if acc is None else acc + part
        off += ka
    cols = pl.ds(pl.multiple_of(j * tn, tn), tn)
    o_scr[:, cols] = acc
    x_scr[:, cols] = x_ref[...]

    @pl.when(j == pl.num_programs(1) - 1)
    def _():
        _post_norm_residual_to_hbm(o_scr, x_scr, g_ref, gn_ref, h_scr, o_hbm, h_hbm,
                                   i * tm, sems)


def _proj_norm_res(a_list, w, x, g, g_next, *, tm, tn):
    m, d = x.shape
    k = w.shape[0]
    n_a = len(a_list)
    nbytes = 2 * tm * k * 2 + 2 * k * tn * 2 + 2 * tm * tn * 4 + 2 * tm * d * 4 + tm * d * 2
    a_specs = [pl.BlockSpec((tm, a.shape[1]), lambda i, j: (i, 0)) for a in a_list]
    return pl.pallas_call(
        functools.partial(_proj_norm_res_kernel, n_a=n_a),
        grid=(m // tm, d // tn),
        in_specs=a_specs + [
            pl.BlockSpec((k, tn), lambda i, j: (0, j)),
            pl.BlockSpec((tm, tn), lambda i, j: (i, j)),
            pl.BlockSpec((1, d), lambda i, j: (0, 0)),
            pl.BlockSpec((1, d), lambda i, j: (0, 0)),
        ],
        out_specs=[pl.BlockSpec(memory_space=pl.ANY), pl.BlockSpec(memory_space=pl.ANY)],
        out_shape=[jax.ShapeDtypeStruct((m, d), F32), jax.ShapeDtypeStruct((m, d), BF16)],
        scratch_shapes=[pltpu.VMEM((tm, d), F32), pltpu.VMEM((tm, d), F32),
                        pltpu.VMEM((tm, d), BF16), pltpu.SemaphoreType.DMA((2,))],
        compiler_params=pltpu.CompilerParams(
            dimension_semantics=("arbitrary", "arbitrary"),
            vmem_limit_bytes=_vmem_limit(nbytes)),
        name="proj_norm_res",
    )(*a_list, w, x, g, g_next)


def _xattn_kernel(h_ref, wq_ref, k_ref, v_ref, o_ref, wqb_ref, *, scale):
    @pl.when(pl.program_id(1) == 0)
    def _():
        wqb_ref[...] = wq_ref[...].astype(BF16)

    q = jnp.dot(h_ref[...], wqb_ref[...], preferred_element_type=F32).astype(BF16)
    s = lax.dot_general(q, k_ref[...], (((1,), (1,)), ((), ())),
                        preferred_element_type=F32) * scale
    p = jnp.exp(s - jnp.max(s, axis=-1, keepdims=True))
    p = (p / jnp.sum(p, axis=-1, keepdims=True)).astype(BF16)
    o_ref[...] = jnp.dot(p, v_ref[...], preferred_element_type=F32).astype(o_ref.dtype)


def _xattn(h, w_q, kv, layer, *, seq, tm):
    m, d = h.shape
    dh = d // XA_HEADS
    nbytes = (2 * tm * d * 2 + d * dh * 4 + d * dh * 2 + 8 * N_MEM * dh * 2 + 2 * tm * dh * 2
              + tm * dh * 6)
    return pl.pallas_call(
        functools.partial(_xattn_kernel, scale=float(dh) ** -0.5),
        grid=(XA_HEADS, m // tm),
        in_specs=[
            pl.BlockSpec((tm, d), lambda j, i: (i, 0)),
            pl.BlockSpec((None, d, dh), lambda j, i: (layer, 0, j), **_W_ONCE),
            pl.BlockSpec((N_MEM, dh), lambda j, i: ((i * tm) // seq, j)),
            pl.BlockSpec((N_MEM, dh), lambda j, i: ((i * tm) // seq, j + XA_HEADS)),
        ],
        out_specs=pl.BlockSpec((tm, dh), lambda j, i: (i, j)),
        out_shape=jax.ShapeDtypeStruct((m, d), BF16),
        scratch_shapes=[pltpu.VMEM((d, dh), BF16)],
        compiler_params=pltpu.CompilerParams(
            dimension_semantics=("arbitrary", "arbitrary"),
            vmem_limit_bytes=_vmem_limit(nbytes)),
        name="xattn",
    )(h, w_q, kv, kv)


def _mlp_kernel(*refs, n_cast):
    if n_cast:
        h_ref, x_hbm, wu_ref, wd_ref, g_ref, gn_ref = refs[:6]
        cast_in = refs[6:6 + n_cast]
        o_hbm, hn_hbm = refs[6 + n_cast:8 + n_cast]
        cast_out = refs[8 + n_cast:8 + 2 * n_cast]
        o_scr, xbuf, hbuf, sems = refs[8 + 2 * n_cast:]
    else:
        (h_ref, x_hbm, wu_ref, wd_ref, g_ref, o_hbm, o_scr, xbuf, sems) = refs
        gn_ref = hn_hbm = hbuf = None
        cast_in = cast_out = ()
    i = pl.program_id(0)
    f = pl.program_id(1)
    last = pl.num_programs(1) - 1
    cp = _EpilogueCopies(o_scr, x_hbm, o_hbm, hn_hbm, xbuf, hbuf, sems, i * o_scr.shape[0])

    @pl.when(f == 0)
    def _():
        o_scr[...] = jnp.zeros(o_scr.shape, F32)

    @pl.when(f == last)
    def _():
        for r in range(X_SLOTS):
            cp.x_in(r, r).start()

    for src, dst in zip(cast_in, cast_out):
        dst[...] = src[...].astype(dst.dtype)
    u = jnp.dot(h_ref[...], wu_ref[...], preferred_element_type=F32)
    u = jnp.square(jnp.maximum(u, 0.0)).astype(BF16)
    o_scr[...] += jnp.dot(u, wd_ref[...], preferred_element_type=F32)

    @pl.when(f == last)
    def _():
        _post_norm_residual_streamed(cp, g_ref, gn_ref)


def _streamed_epilogue_scratch(tm, d, emit_h):
    scratch = [pltpu.VMEM((tm, d), F32), pltpu.VMEM((X_SLOTS, NORM_ROWS, d), F32)]
    if emit_h:
        scratch.append(pltpu.VMEM((2, NORM_ROWS, d), BF16))
    return scratch + [pltpu.SemaphoreType.DMA((X_SLOTS + 3,))]


def _mlp(h, x, w_up, w_down, g_post, nxt, *, tm, tf):
    m, d = x.shape
    d_ff = w_up.shape[1]
    ni, nf = m // tm, d_ff // tf
    nbytes = ((2 if nxt is None else 1) * tm * d * 2 + 4 * d * tf * 2 + tm * d * 4
              + (X_SLOTS + 1) * NORM_ROWS * d * 4 + tm * tf * 6)
    h_buffers = pl.Buffered(2 if nxt is None else 1)
    in_specs = [
        pl.BlockSpec((tm, d), lambda i, f: (i, 0), pipeline_mode=h_buffers),
        pl.BlockSpec(memory_space=pl.ANY),
        pl.BlockSpec((d, tf), lambda i, f: (0, f)),
        pl.BlockSpec((tf, d), lambda i, f: (f, 0)),
        pl.BlockSpec((1, d), lambda i, f: (0, 0)),
    ]
    out_specs = [pl.BlockSpec(memory_space=pl.ANY)]
    out_shape = [jax.ShapeDtypeStruct((m, d), F32)]
    args = [h, x, w_up, w_down, g_post]
    n_cast = 0
    if nxt is not None:
        g_next, next_layer, cast_ws = nxt
        n_cast = len(cast_ws)
        in_specs.append(pl.BlockSpec((1, d), lambda i, f: (0, 0)))
        args.append(g_next)
        out_specs.append(pl.BlockSpec(memory_space=pl.ANY))
        out_shape.append(jax.ShapeDtypeStruct((m, d), BF16))
        for w in cast_ws:
            _, r, c = w.shape
            br, bc = r // ni, c // nf
            assert br * ni == r and bc * nf == c and br % 16 == 0 and bc % LANES == 0
            in_specs.append(pl.BlockSpec((None, br, bc), lambda i, f: (next_layer, i, f)))
            args.append(w)
            out_specs.append(pl.BlockSpec((br, bc), lambda i, f: (i, f)))
            out_shape.append(jax.ShapeDtypeStruct((r, c), BF16))
            nbytes += 2 * br * bc * 6
    outs = pl.pallas_call(
        functools.partial(_mlp_kernel, n_cast=n_cast),
        grid=(ni, nf),
        in_specs=in_specs,
        out_specs=out_specs,
        out_shape=out_shape,
        scratch_shapes=_streamed_epilogue_scratch(tm, d, emit_h=nxt is not None),
        compiler_params=pltpu.CompilerParams(
            dimension_semantics=("arbitrary", "arbitrary"),
            vmem_limit_bytes=_vmem_limit(nbytes)),
        name="mlp",
    )(*args)
    if nxt is None:
        return outs[0], None, []
    return outs[0], outs[1], list(outs[2:])


def kernel(x, mem, g_pre_mix, w_in, conv_w, conv_b, ln_a_g, ln_a_b, ln_v_g, ln_v_b, w_spatial, b_spatial, w_out, g_post_mix, g_pre_xa, g_mem, w_q, w_kv, w_o, g_post_xa, g_pre_mlp, w_up, w_down, g_post_mlp):
    batch, seq, d = x.shape
    depth = w_in.shape[0]
    w_a = conv_w.shape[2]
    w_b = ln_v_g.shape[1]
    m = batch * seq

    restreamed = [w_out, w_o, w_up, w_down]
    w_out_b, w_o_b, w_up_b, w_down_b = [w[0].astype(BF16) for w in restreamed]

    xf = x.reshape(m, d)
    memf = mem.reshape(batch * N_MEM, d)

    def row(p, l):
        return p[l].reshape(1, -1)

    h = _prenorm(xf, row(g_pre_mix, 0), tm=512)
    for l in range(depth):
        hglu = _inproj_glu(h, w_in, l, w_a=w_a, tm=1024, tn=512)
        y_a = _conv_ln_silu(hglu, conv_w[l], row(conv_b, l), row(ln_a_g, l), row(ln_a_b, l),
                            batch=batch, seq=seq)
        y_b = _inproj_gmlp(h, w_in, row(ln_v_g, l), row(ln_v_b, l), w_spatial,
                           b_spatial[l].reshape(H_B, CHUNK, 1), l, w_a=w_a, w_b=w_b, tm=1024)
        xf, h = _proj_norm_res([y_a, y_b], w_out_b, xf, row(g_post_mix, l),
                               row(g_pre_xa, l), tm=512, tn=1024)
        kv = _norm_matmul(memf, row(g_mem, l), w_kv, l, tm=batch * N_MEM, tn=512,
                          out_dtype=BF16)
        o = _xattn(h, w_q, kv, l, seq=seq, tm=1024)
        xf, h = _proj_norm_res([o], w_o_b, xf, row(g_post_xa, l), row(g_pre_mlp, l),
                               tm=512, tn=1024)
        nxt = (row(g_pre_mix, l + 1), l + 1, restreamed) if l + 1 < depth else None
        xf, h, cast = _mlp(h, xf, w_up_b, w_down_b, row(g_post_mlp, l), nxt, tm=1024, tf=512)
        if cast:
            w_out_b, w_o_b, w_up_b, w_down_b = cast
    return xf.reshape(batch, seq, d)
```
